```python
import jax, jax.numpy as jnp
from jax import lax
import numpy as np

D_MODEL = 1024
BATCH = 2
SEQ = 8192
DEPTH = 2
DEC_BATCH = 32
DEC_SEQ = 4
PAST_LEN = 8192
PAGE_SIZE = 128

HEAD_DIM = 64
N_SB_HEADS = 8
N_DSA_HEADS = 8
N_EVEN_HEADS = N_SB_HEADS + N_DSA_HEADS
N_IDX_HEADS = 4
IDX_DIM = 64
DSA_TOPK = 256
N_FOX_HEADS = D_MODEL // HEAD_DIM
D_FF = 2816
ROPE_THETA = 10000.0
Q_BLOCK = 128
N_SUB = 3
N_EVEN = (DEPTH + 1) // 2
N_ODD = DEPTH // 2
ALPHA = (2 * DEPTH) ** 0.25
BETA_INIT = (8 * DEPTH) ** -0.25
LN_EPS = 1e-5
W_SB = N_SB_HEADS * HEAD_DIM
W_DSA = N_DSA_HEADS * HEAD_DIM
P_EVEN = 3 * W_SB + 3 * W_DSA + N_IDX_HEADS * IDX_DIM + IDX_DIM + N_IDX_HEADS
P_ODD = 3 * D_MODEL + N_FOX_HEADS

kernel_name = 'stick_dsa_fox_hybrid_step'


def layer_norm(x, g, b):
    xf = x.astype(jnp.float32)
    mu = xf.mean(-1, keepdims=True)
    var = jnp.square(xf - mu).mean(-1, keepdims=True)
    return ((xf - mu) * lax.rsqrt(var + LN_EPS) * g + b).astype(x.dtype)


def rotary(x, pos):
    half = x.shape[-1] // 2
    inv = ROPE_THETA ** (-jnp.arange(half, dtype=jnp.float32) / half)
    ang = pos.astype(jnp.float32)[:, None] * inv
    if x.ndim == 4:
        ang = ang[:, None, :]
    cos, sin = jnp.cos(ang), jnp.sin(ang)
    x1, x2 = x[..., :half], x[..., half:]
    return jnp.concatenate([x1 * cos - x2 * sin, x2 * cos + x1 * sin], axis=-1).astype(x.dtype)


def swiglu(h, w1, w3, w2):
    return (jax.nn.silu(h @ w1) * (h @ w3)) @ w2


def map_query_blocks(fn, q_pos, *q_arrays):
    t = q_pos.shape[0]
    blk = min(Q_BLOCK, t)
    nb = t // blk
    def split(a):
        return jnp.moveaxis(a.reshape(a.shape[0], nb, blk, *a.shape[2:]), 1, 0)
    out = lax.map(lambda args: fn(*args), (q_pos.reshape(nb, blk), *[split(a) for a in q_arrays]))
    out = jnp.moveaxis(out, 0, 1)
    return out.reshape(out.shape[0], t, *out.shape[3:])


def stick_breaking_attention(q, k, v, q_pos, k_pos):
    scale = q.shape[-1] ** -0.5
    def block(qp, qb):
        z = jnp.einsum('bqhd,bkhd->bhqk', qb, k, preferred_element_type=jnp.float32) * scale
        visible = k_pos[None, :] < qp[:, None]
        log_1m = jnp.where(visible, -jax.nn.softplus(z), 0.0)
        rest = lax.cumsum(log_1m, axis=3, reverse=True) - log_1m
        w = jnp.where(visible, jnp.exp(jax.nn.log_sigmoid(z) + rest), 0.0)
        return jnp.einsum('bhqk,bkhd->bqhd', w.astype(v.dtype), v)
    return map_query_blocks(block, q_pos, q)


def dsa_attention(q, k, v, q_idx, k_idx, w_idx, q_pos, k_pos):
    top = min(DSA_TOPK, k.shape[1] // 4)
    scale = q.shape[-1] ** -0.5
    def block(qp, qb, qib, wb):
        s_idx = jnp.einsum('bqhd,bkd->bqhk', qib, k_idx, preferred_element_type=jnp.float32) * IDX_DIM ** -0.5
        score = jnp.einsum('bqh,bqhk->bqk', wb.astype(jnp.float32) * N_IDX_HEADS ** -0.5, jax.nn.relu(s_idx))
        visible = k_pos[None, :] <= qp[:, None]
        score = jnp.where(visible[None], score, -jnp.inf)
        _, sel = lax.top_k(score, top)
        k_sel = jax.vmap(lambda kk, ii: kk[ii])(k, sel)
        v_sel = jax.vmap(lambda vv, ii: vv[ii])(v, sel)
        ok = k_pos[sel] <= qp[None, :, None]
        logit = jnp.einsum('bqhd,bqkhd->bhqk', qb, k_sel, preferred_element_type=jnp.float32) * scale
        p = jax.nn.softmax(jnp.where(ok[:, None], logit, -jnp.inf), axis=-1)
        return jnp.einsum('bhqk,bqkhd->bqhd', p.astype(v.dtype), v_sel)
    return map_query_blocks(block, q_pos, q, q_idx, w_idx)


def forgetting_attention(q, k, v, cum_q, cum_k, q_pos, k_pos):
    scale = q.shape[-1] ** -0.5
    cum_k_t = jnp.swapaxes(cum_k, 1, 2)
    def block(qp, qb, fq):
        logit = jnp.einsum('bqhd,bkhd->bhqk', qb, k, preferred_element_type=jnp.float32) * scale
        logit = logit + jnp.swapaxes(fq, 1, 2)[..., None] - cum_k_t[:, :, None, :]
        visible = k_pos[None, :] <= qp[:, None]
        p = jax.nn.softmax(jnp.where(visible, logit, -jnp.inf), axis=-1)
        return jnp.einsum('bhqk,bkhd->bqhd', p.astype(v.dtype), v)
    return map_query_blocks(block, q_pos, q, cum_q)


def even_mixer(h, pos, past, w_in, w_out):
    b, t, _ = h.shape
    sizes = [W_SB, W_SB, W_SB, W_DSA, W_DSA, W_DSA, N_IDX_HEADS * IDX_DIM, IDX_DIM, N_IDX_HEADS]
    parts = jnp.split(h @ w_in, np.cumsum(sizes)[:-1].tolist(), axis=-1)
    def heads(a, n):
        return a.reshape(b, t, n, -1)
    q_sb = heads(parts[0], N_SB_HEADS)
    k_sb = heads(parts[1], N_SB_HEADS)
    v_sb = heads(parts[2], N_SB_HEADS)
    q_ds = rotary(heads(parts[3], N_DSA_HEADS), pos)
    k_ds = rotary(heads(parts[4], N_DSA_HEADS), pos)
    v_ds = heads(parts[5], N_DSA_HEADS)
    q_ix = rotary(heads(parts[6], N_IDX_HEADS), pos)
    k_ix = rotary(parts[7], pos)
    w_ix = parts[8]
    k_new = jnp.concatenate([k_sb, k_ds], axis=2)
    v_new = jnp.concatenate([v_sb, v_ds], axis=2)
    if past is None:
        k_all, v_all, kix_all, k_pos = k_new, v_new, k_ix, pos
    else:
        past_k, past_v, past_kix = past
        k_all = jnp.concatenate([past_k, k_new], axis=1)
        v_all = jnp.concatenate([past_v, v_new], axis=1)
        kix_all = jnp.concatenate([past_kix, k_ix], axis=1)
        k_pos = jnp.arange(k_all.shape[1], dtype=jnp.int32)
    o_sb = stick_breaking_attention(q_sb, k_all[:, :, :N_SB_HEADS], v_all[:, :, :N_SB_HEADS], pos, k_pos)
    o_ds = dsa_attention(q_ds, k_all[:, :, N_SB_HEADS:], v_all[:, :, N_SB_HEADS:], q_ix, kix_all, w_ix, pos, k_pos)
    o = jnp.concatenate([o_sb, o_ds], axis=2).reshape(b, t, D_MODEL)
    return o @ w_out, (k_new, v_new, k_ix)


def odd_mixer(h, pos, past, w_in, b_f, w_out):
    b, t, _ = h.shape
    parts = jnp.split(h @ w_in, [D_MODEL, 2 * D_MODEL, 3 * D_MODEL], axis=-1)
    q = parts[0].reshape(b, t, N_FOX_HEADS, HEAD_DIM)
    k = parts[1].reshape(b, t, N_FOX_HEADS, HEAD_DIM)
    v = parts[2].reshape(b, t, N_FOX_HEADS, HEAD_DIM)
    logf = jax.nn.log_sigmoid((parts[3] + b_f).astype(jnp.float32))
    if past is None:
        k_all, v_all, logf_all, k_pos = k, v, logf, pos
    else:
        past_k, past_v, past_logf = past
        k_all = jnp.concatenate([past_k, k], axis=1)
        v_all = jnp.concatenate([past_v, v], axis=1)
        logf_all = jnp.concatenate([past_logf.astype(jnp.float32), logf], axis=1)
        k_pos = jnp.arange(k_all.shape[1], dtype=jnp.int32)
    cum_k = lax.cumsum(logf_all, axis=1)
    cum_q = cum_k[:, -t:]
    o = forgetting_attention(q, k_all, v_all, cum_q, cum_k, pos, k_pos).reshape(b, t, D_MODEL)
    return o @ w_out, (k, v, logf)


def gather_pages(cache, layer_idx, page_table):
    rows = cache[layer_idx, page_table]
    return rows.reshape(page_table.shape[0], page_table.shape[1] * PAGE_SIZE, *cache.shape[3:])


def modulate(x, m):
    return x * (1 + m[:, 1]) + m[:, 0]


def post_norm(x, branch, m, g, bias):
    return layer_norm(ALPHA * x + m[:, 2] * branch, g, bias)


def trunk(x, c, pos, past, w_ada, b_ada, ln_g, ln_b, ffn_w1, ffn_w3, ffn_w2,
          w_in_even, w_out_even, w_in_odd, b_forget, w_out_odd):
    b = x.shape[0]
    even_rows, odd_rows = [], []
    for layer in range(DEPTH):
        j = layer // 2
        mod = (jax.nn.silu(c) @ w_ada[layer] + b_ada[layer]).reshape(b, N_SUB, 3, 1, D_MODEL)
        h = modulate(x, mod[:, 0])
        x = post_norm(x, 0.5 * swiglu(h, ffn_w1[layer, 0], ffn_w3[layer, 0], ffn_w2[layer, 0]),
                      mod[:, 0], ln_g[layer, 0], ln_b[layer, 0])
        h = modulate(x, mod[:, 1])
        if layer % 2 == 0:
            past_j = None
            if past is not None:
                page_table = past[6]
                past_j = (gather_pages(past[0], j, page_table), gather_pages(past[1], j, page_table),
                          gather_pages(past[2], j, page_table))
            out, rows = even_mixer(h, pos, past_j, w_in_even[j], w_out_even[j])
            even_rows.append(rows)
        else:
            past_j = None
            if past is not None:
                page_table = past[6]
                past_j = (gather_pages(past[3], j, page_table), gather_pages(past[4], j, page_table),
                          gather_pages(past[5], j, page_table))
            out, rows = odd_mixer(h, pos, past_j, w_in_odd[j], b_forget[j], w_out_odd[j])
            odd_rows.append(rows)
        x = post_norm(x, out, mod[:, 1], ln_g[layer, 1], ln_b[layer, 1])
        h = modulate(x, mod[:, 2])
        x = post_norm(x, 0.5 * swiglu(h, ffn_w1[layer, 1], ffn_w3[layer, 1], ffn_w2[layer, 1]),
                      mod[:, 2], ln_g[layer, 2], ln_b[layer, 2])
    return x, even_rows, odd_rows


def setup_inputs(seed: int = 0) -> dict:
    key = jax.random.key(seed)
    ks = jax.random.split(key, 23)
    n_pages = PAST_LEN // PAGE_SIZE
    n_used = DEC_BATCH * n_pages
    n_pool = n_used + max(1, n_used // 4)
    def nrm(k, shape, s=1.0):
        return s * jax.random.normal(k, shape, jnp.float32)
    page_table = jax.random.permutation(ks[8], n_pool)[:n_used].reshape(DEC_BATCH, n_pages).astype(jnp.int32)
    return {
        'x_prompt': nrm(ks[0], (BATCH, SEQ, D_MODEL)),
        'x_sample': nrm(ks[1], (DEC_BATCH, DEC_SEQ, D_MODEL)),
        'cache_k_even': nrm(ks[2], (N_EVEN, n_pool, PAGE_SIZE, N_EVEN_HEADS, HEAD_DIM)),
        'cache_v_even': nrm(ks[3], (N_EVEN, n_pool, PAGE_SIZE, N_EVEN_HEADS, HEAD_DIM)),
        'cache_kidx_even': nrm(ks[4], (N_EVEN, n_pool, PAGE_SIZE, IDX_DIM)),
        'cache_k_odd': nrm(ks[5], (N_ODD, n_pool, PAGE_SIZE, N_FOX_HEADS, HEAD_DIM)),
        'cache_v_odd': nrm(ks[6], (N_ODD, n_pool, PAGE_SIZE, N_FOX_HEADS, HEAD_DIM)),
        'cache_logf_odd': jax.nn.log_sigmoid(2.0 + nrm(ks[7], (N_ODD, n_pool, PAGE_SIZE, N_FOX_HEADS))),
        'page_table': page_table,
        'c_prompt': nrm(ks[9], (BATCH, D_MODEL)),
        'c_sample': nrm(ks[10], (DEC_BATCH, D_MODEL)),
        'w_ada': nrm(ks[11], (DEPTH, D_MODEL, N_SUB * 3 * D_MODEL), D_MODEL ** -0.5),
        'b_ada': nrm(ks[12], (DEPTH, N_SUB * 3 * D_MODEL), 0.02),
        'ln_g': 1.0 + nrm(ks[13], (DEPTH, N_SUB, D_MODEL), 0.02),
        'ln_b': nrm(ks[14], (DEPTH, N_SUB, D_MODEL), 0.02),
        'ffn_w1': nrm(ks[15], (DEPTH, 2, D_MODEL, D_FF), D_MODEL ** -0.5),
        'ffn_w3': nrm(ks[16], (DEPTH, 2, D_MODEL, D_FF), D_MODEL ** -0.5),
        'ffn_w2': nrm(ks[17], (DEPTH, 2, D_FF, D_MODEL), BETA_INIT * D_FF ** -0.5),
        'w_in_even': nrm(ks[18], (N_EVEN, D_MODEL, P_EVEN), D_MODEL ** -0.5),
        'w_out_even': nrm(ks[19], (N_EVEN, D_MODEL, D_MODEL), BETA_INIT * D_MODEL ** -0.5),
        'w_in_odd': nrm(ks[20], (N_ODD, D_MODEL, P_ODD), D_MODEL ** -0.5),
        'b_forget': 2.0 + nrm(ks[21], (N_ODD, N_FOX_HEADS), 0.1),
        'w_out_odd': nrm(ks[22], (N_ODD, D_MODEL, D_MODEL), BETA_INIT * D_MODEL ** -0.5),
    }


def reference(x_prompt, x_sample, cache_k_even, cache_v_even, cache_kidx_even, cache_k_odd,
              cache_v_odd, cache_logf_odd, page_table, c_prompt, c_sample, w_ada, b_ada, ln_g,
              ln_b, ffn_w1, ffn_w3, ffn_w2, w_in_even, w_out_even, w_in_odd, b_forget, w_out_odd):
    pos_prompt = jnp.arange(x_prompt.shape[1], dtype=jnp.int32)
    pos_sample = PAST_LEN + jnp.arange(x_sample.shape[1], dtype=jnp.int32)
    y_prompt, ev_p, od_p = trunk(x_prompt, c_prompt, pos_prompt, None, w_ada, b_ada, ln_g, ln_b,
                                 ffn_w1, ffn_w3, ffn_w2, w_in_even, w_out_even, w_in_odd,
                                 b_forget, w_out_odd)
    past = (cache_k_even, cache_v_even, cache_kidx_even, cache_k_odd, cache_v_odd,
            cache_logf_odd, page_table)
    y_sample, ev_s, od_s = trunk(x_sample, c_sample, pos_sample, past, w_ada, b_ada, ln_g, ln_b,
                                 ffn_w1, ffn_w3, ffn_w2, w_in_even, w_out_even, w_in_odd,
                                 b_forget, w_out_odd)
    def stack(rows, i):
        return jnp.stack([r[i] for r in rows])
    return (y_prompt, y_sample,
            stack(ev_p, 0), stack(ev_p, 1), stack(ev_p, 2),
            stack(od_p, 0), stack(od_p, 1), stack(od_p, 2),
            stack(ev_s, 0), stack(ev_s, 1), stack(ev_s, 2),
            stack(od_s, 0), stack(od_s, 1), stack(od_s, 2))
```

```python
import functools

import jax
import jax.numpy as jnp
from jax import lax
from jax.experimental import pallas as pl
from jax.experimental.pallas import tpu as pltpu

F32 = jnp.float32
BF16 = jnp.bfloat16
I32 = jnp.int32

D_MODEL = 1024
HEAD_DIM = 64
N_HEADS = 16
N_SB_HEADS = 8
N_PAIRS = N_HEADS // 2
N_SB_PAIRS = N_SB_HEADS // 2
W_HALF = N_SB_HEADS * HEAD_DIM
N_IDX_HEADS = 4
IDX_DIM = 64
DSA_TOPK = 256
D_FF = 2816
ROPE_THETA = 10000.0
PAGE = 128
N_SUB = 3
DEPTH = 2
ALPHA = (2 * DEPTH) ** 0.25
LN_EPS = 1e-5
LANES = 128
P_EVEN = 6 * W_HALF + N_IDX_HEADS * IDX_DIM + IDX_DIM + N_IDX_HEADS
P_EVEN_PAD = 3456
IDX_COL = 6 * W_HALF
IDX_W = P_EVEN_PAD - IDX_COL
P_ODD_PAD = 3 * D_MODEL + LANES
NEG = -1e30
SB_SKIP = -105.0
INT_MIN = -(2 ** 31)
VMEM_LIMIT = 56 * 1024 * 1024


def _cparams(sem):
    return pltpu.CompilerParams(dimension_semantics=sem, vmem_limit_bytes=VMEM_LIMIT)


def _dot(a, b):
    return jnp.dot(a, b, preferred_element_type=F32)


def _dot_nt(a, b):
    return lax.dot_general(a, b, (((1,), (1,)), ((), ())), preferred_element_type=F32)


def _split3(x):
    hi = x.astype(BF16)
    r = x - hi.astype(F32)
    mid = r.astype(BF16)
    lo = (r - mid.astype(F32)).astype(BF16)
    return hi, mid, lo


def _dot3_right(x, u):
    hi, mid, lo = _split3(x)
    return _dot(hi, u) + _dot(mid, u) + _dot(lo, u)


def _dot3_left(u, x):
    hi, mid, lo = _split3(x)
    return _dot(u, hi) + _dot(u, mid) + _dot(u, lo)


def _tri(n, rel):
    r = lax.broadcasted_iota(I32, (n, n), 0)
    c = lax.broadcasted_iota(I32, (n, n), 1)
    m = {"gt": r > c, "lt": r < c, "ge": r >= c}[rel]
    return jnp.where(m, 1.0, 0.0).astype(BF16)


def _layer_norm(y, g, b):
    mu = jnp.mean(y, axis=-1, keepdims=True)
    d = y - mu
    var = jnp.mean(d * d, axis=-1, keepdims=True)
    return d * lax.rsqrt(var + LN_EPS) * g + b


def _softplus_tail(z):
    return jnp.log1p(jnp.exp(-jnp.abs(z)))


def _mod_spec(arr, tm, n_grid):
    if arr.shape[1] == 1:
        if n_grid == 2:
            return pl.BlockSpec((1, 1, D_MODEL), lambda g, t: (g, 0, 0))
        return pl.BlockSpec((1, 1, D_MODEL), lambda g, t, k: (g, 0, 0))
    if n_grid == 2:
        return pl.BlockSpec((1, tm, D_MODEL), lambda g, t: (g, t, 0))
    return pl.BlockSpec((1, tm, D_MODEL), lambda g, t, k: (g, t, 0))


def _ada_kernel(c_ref, w_ref, b_ref, o_ref):
    c = c_ref[...]
    a = (c * jax.nn.sigmoid(c)).astype(BF16)
    o_ref[0] = _dot(a, w_ref[0].astype(BF16)) + b_ref[0]


def _ada_mod(c_all, w_ada, b_ada):
    m = c_all.shape[0]
    depth, _, n = w_ada.shape
    tn = 1152
    return pl.pallas_call(
        _ada_kernel,
        grid=(depth, n // tn),
        in_specs=[pl.BlockSpec((m, D_MODEL), lambda l, j: (0, 0)),
                  pl.BlockSpec((1, D_MODEL, tn), lambda l, j: (l, 0, j)),
                  pl.BlockSpec((1, 1, tn), lambda l, j: (l, 0, j))],
        out_specs=pl.BlockSpec((1, m, tn), lambda l, j: (l, 0, j)),
        out_shape=jax.ShapeDtypeStruct((depth, m, n), F32),
        compiler_params=_cparams(("parallel", "parallel")),
        name="ada_mod",
    )(c_all, w_ada, b_ada.reshape(depth, 1, n))


def _ffn_kernel(x_ref, sh_ref, sc_ref, gt_ref, w1_ref, w3_ref, w2_ref, g_ref, b_ref, o_ref,
                h_sc, acc_sc):
    k = pl.program_id(2)

    @pl.when(k == 0)
    def _():
        h_sc[...] = (x_ref[0] * (1.0 + sc_ref[0]) + sh_ref[0]).astype(BF16)
        acc_sc[...] = jnp.zeros_like(acc_sc)

    h = h_sc[...]
    a = _dot(h, w1_ref[...])
    b = _dot(h, w3_ref[...])
    act = (a * jax.nn.sigmoid(a) * b).astype(BF16)
    acc_sc[...] += _dot(act, w2_ref[...])

    @pl.when(k == pl.num_programs(2) - 1)
    def _():
        y = ALPHA * x_ref[0] + gt_ref[0] * (0.5 * acc_sc[...])
        o_ref[0] = _layer_norm(y, g_ref[...], b_ref[...])


def _ffn_sublayer(x, shift, scale, gate, w1, w3, w2, ln_g, ln_b):
    g, tg, _ = x.shape
    tm = min(512, tg)
    tf = D_FF // 2
    xs = pl.BlockSpec((1, tm, D_MODEL), lambda g_, t, k: (g_, t, 0))
    vec = pl.BlockSpec((1, D_MODEL), lambda g_, t, k: (0, 0))
    return pl.pallas_call(
        _ffn_kernel,
        grid=(g, tg // tm, D_FF // tf),
        in_specs=[xs, _mod_spec(shift, tm, 3), _mod_spec(scale, tm, 3), _mod_spec(gate, tm, 3),
                  pl.BlockSpec((D_MODEL, tf), lambda g_, t, k: (0, k)),
                  pl.BlockSpec((D_MODEL, tf), lambda g_, t, k: (0, k)),
                  pl.BlockSpec((tf, D_MODEL), lambda g_, t, k: (k, 0)),
                  vec, vec],
        out_specs=xs,
        out_shape=jax.ShapeDtypeStruct(x.shape, F32),
        scratch_shapes=[pltpu.VMEM((tm, D_MODEL), BF16), pltpu.VMEM((tm, D_MODEL), F32)],
        compiler_params=_cparams(("parallel", "parallel", "arbitrary")),
        name="ffn_sublayer",
    )(x, shift, scale, gate, w1, w3, w2, ln_g.reshape(1, D_MODEL), ln_b.reshape(1, D_MODEL))


def _rotate(x, cos, sin_signed):
    n = x.shape[-1]
    lane = lax.broadcasted_iota(I32, x.shape, 1)
    first = (lane % HEAD_DIM) < (HEAD_DIM // 2)
    other = jnp.where(first, pltpu.roll(x, n - HEAD_DIM // 2, 1), pltpu.roll(x, HEAD_DIM // 2, 1))
    return x * cos[:, :n] + other * sin_signed[:, :n]


def _even_proj_kernel(x_ref, sh_ref, sc_ref, w_ref, cos_ref, sin_ref,
                      k_ref, v_ref, kix_ref, qb_ref, kt_ref, vb_ref, qix_ref, wix_ref, kixb_ref):
    h = (x_ref[0] * (1.0 + sc_ref[0]) + sh_ref[0]).astype(BF16)
    cos = cos_ref[...]
    sin = sin_ref[...]

    def proj(lo, width):
        return _dot(h, w_ref[:, lo:lo + width])

    scale = HEAD_DIM ** -0.5
    q_sb = proj(0, W_HALF)
    qb_ref[0, :, 0:W_HALF] = (q_sb * scale).astype(BF16)
    k_sb = proj(W_HALF, W_HALF)
    k_ref[0, :, 0:W_HALF] = k_sb
    kt_ref[0, 0:W_HALF, :] = k_sb.T.astype(BF16)
    v_sb = proj(2 * W_HALF, W_HALF)
    v_ref[0, :, 0:W_HALF] = v_sb
    vb_ref[0, :, 0:W_HALF] = v_sb.astype(BF16)
    q_ds = _rotate(proj(3 * W_HALF, W_HALF), cos, sin)
    qb_ref[0, :, W_HALF:2 * W_HALF] = (q_ds * scale).astype(BF16)
    k_ds = _rotate(proj(4 * W_HALF, W_HALF), cos, sin)
    k_ref[0, :, W_HALF:2 * W_HALF] = k_ds
    kt_ref[0, W_HALF:2 * W_HALF, :] = k_ds.T.astype(BF16)
    v_ds = proj(5 * W_HALF, W_HALF)
    v_ref[0, :, W_HALF:2 * W_HALF] = v_ds
    vb_ref[0, :, W_HALF:2 * W_HALF] = v_ds.astype(BF16)
    ix = proj(IDX_COL, IDX_W)
    ixr = _rotate(ix, cos, sin)
    nq = N_IDX_HEADS * IDX_DIM
    qix_ref[0] = (ixr[:, 0:nq] * IDX_DIM ** -0.5).astype(BF16)
    kix = ixr[:, nq:nq + IDX_DIM]
    kix_ref[0] = kix
    kixb_ref[0] = kix.astype(BF16)
    wix_ref[0] = ix[:, nq + IDX_DIM:nq + IDX_DIM + N_IDX_HEADS] * N_IDX_HEADS ** -0.5


def _even_proj(x, shift, scale, w_pad, cos, sin):
    g, tg, _ = x.shape
    tm = min(512, tg)
    nt = tg // tm
    row = lambda width: pl.BlockSpec((1, tm, width), lambda g_, t: (g_, t, 0))
    tab = pl.BlockSpec((tm, W_HALF), lambda g_, t: (g_ * nt + t, 0))
    shp = lambda width, dt: jax.ShapeDtypeStruct((g, tg, width), dt)
    return pl.pallas_call(
        _even_proj_kernel,
        grid=(g, nt),
        in_specs=[row(D_MODEL), _mod_spec(shift, tm, 2), _mod_spec(scale, tm, 2),
                  pl.BlockSpec((D_MODEL, P_EVEN_PAD), lambda g_, t: (0, 0)), tab, tab],
        out_specs=[row(D_MODEL), row(D_MODEL), row(IDX_DIM), row(D_MODEL),
                   pl.BlockSpec((1, D_MODEL, tm), lambda g_, t: (g_, 0, t)),
                   row(D_MODEL), row(N_IDX_HEADS * IDX_DIM), row(N_IDX_HEADS), row(IDX_DIM)],
        out_shape=[shp(D_MODEL, F32), shp(D_MODEL, F32), shp(IDX_DIM, F32), shp(D_MODEL, BF16),
                   jax.ShapeDtypeStruct((g, D_MODEL, tg), BF16), shp(D_MODEL, BF16),
                   shp(N_IDX_HEADS * IDX_DIM, BF16), shp(N_IDX_HEADS, F32), shp(IDX_DIM, BF16)],
        compiler_params=_cparams(("parallel", "parallel")),
        name="even_proj",
    )(x, shift, scale, w_pad, cos, sin)


def _odd_proj_kernel(x_ref, sh_ref, sc_ref, w_ref, bf_ref,
                     k_ref, v_ref, lf_ref, qb_ref, kt_ref, vb_ref, fq_ref, ft_ref, carry_sc):
    t = pl.program_id(1)
    tm = x_ref.shape[1]
    h = (x_ref[0] * (1.0 + sc_ref[0]) + sh_ref[0]).astype(BF16)

    def proj(lo, width):
        return _dot(h, w_ref[:, lo:lo + width])

    q = proj(0, D_MODEL)
    qb_ref[0] = (q * HEAD_DIM ** -0.5).astype(BF16)
    k = proj(D_MODEL, D_MODEL)
    k_ref[0] = k
    kt_ref[0] = k.T.astype(BF16)
    v = proj(2 * D_MODEL, D_MODEL)
    v_ref[0] = v
    vb_ref[0] = v.astype(BF16)
    zf = proj(3 * D_MODEL, LANES) + bf_ref[...]
    logf = jnp.minimum(zf, 0.0) - _softplus_tail(zf)
    lf_ref[0] = logf[:, 0:N_HEADS]

    @pl.when(t == 0)
    def _():
        carry_sc[...] = jnp.zeros_like(carry_sc)

    cum = _dot3_left(_tri(tm, "ge"), logf) + carry_sc[...]
    carry_sc[...] = cum[tm - 1:tm, :]
    fq_ref[0] = cum[:, 0:N_HEADS]
    ft_ref[0] = cum.T[0:N_HEADS, :]


def _odd_proj(x, shift, scale, w_pad, bf_pad):
    g, tg, _ = x.shape
    tm = min(512, tg)
    nt = tg // tm
    row = lambda width: pl.BlockSpec((1, tm, width), lambda g_, t: (g_, t, 0))
    shp = lambda width, dt: jax.ShapeDtypeStruct((g, tg, width), dt)
    return pl.pallas_call(
        _odd_proj_kernel,
        grid=(g, nt),
        in_specs=[row(D_MODEL), _mod_spec(shift, tm, 2), _mod_spec(scale, tm, 2),
                  pl.BlockSpec((D_MODEL, P_ODD_PAD), lambda g_, t: (0, 0)),
                  pl.BlockSpec((1, LANES), lambda g_, t: (0, 0))],
        out_specs=[row(D_MODEL), row(D_MODEL), row(N_HEADS), row(D_MODEL),
                   pl.BlockSpec((1, D_MODEL, tm), lambda g_, t: (g_, 0, t)),
                   row(D_MODEL), row(N_HEADS),
                   pl.BlockSpec((1, N_HEADS, tm), lambda g_, t: (g_, 0, t))],
        out_shape=[shp(D_MODEL, F32), shp(D_MODEL, F32), shp(N_HEADS, F32), shp(D_MODEL, BF16),
                   jax.ShapeDtypeStruct((g, D_MODEL, tg), BF16), shp(D_MODEL, BF16),
                   shp(N_HEADS, F32), jax.ShapeDtypeStruct((g, N_HEADS, tg), F32)],
        scratch_shapes=[pltpu.VMEM((1, LANES), F32)],
        compiler_params=_cparams(("parallel", "arbitrary")),
        name="odd_proj",
    )(x, shift, scale, w_pad, bf_pad)


def _out_proj_kernel(n_parts, *refs):
    o_refs = refs[:n_parts]
    w_refs = refs[n_parts:2 * n_parts]
    x_ref, gt_ref, g_ref, b_ref, y_ref = refs[2 * n_parts:]
    out = _dot(o_refs[0][0], w_refs[0][...])
    for i in range(1, n_parts):
        out = out + _dot(o_refs[i][0], w_refs[i][...])
    y = ALPHA * x_ref[0] + gt_ref[0] * out
    y_ref[0] = _layer_norm(y, g_ref[...], b_ref[...])


def _out_proj(o_parts, w_parts, x, gate, ln_g, ln_b):
    g, tg, _ = x.shape
    tm = min(512, tg)
    n = len(o_parts)
    row = lambda width: pl.BlockSpec((1, tm, width), lambda g_, t: (g_, t, 0))
    vec = pl.BlockSpec((1, D_MODEL), lambda g_, t: (0, 0))
    in_specs = ([row(o.shape[-1]) for o in o_parts]
                + [pl.BlockSpec(w.shape, lambda g_, t: (0, 0)) for w in w_parts]
                + [row(D_MODEL), _mod_spec(gate, tm, 2), vec, vec])
    return pl.pallas_call(
        functools.partial(_out_proj_kernel, n),
        grid=(g, tg // tm),
        in_specs=in_specs,
        out_specs=row(D_MODEL),
        out_shape=jax.ShapeDtypeStruct(x.shape, F32),
        compiler_params=_cparams(("parallel", "parallel")),
        name="out_proj",
    )(*o_parts, *w_parts, x, gate, ln_g.reshape(1, D_MODEL), ln_b.reshape(1, D_MODEL))


def _half_masks(shape):
    lane = lax.broadcasted_iota(I32, shape, 1)
    return lane < HEAD_DIM, lane >= HEAD_DIM


def _sb_kernel(q_ref, kt_ref, v_ref, o_ref):
    tq = q_ref.shape[1]
    tk = tq
    i = pl.program_id(2)
    q = q_ref[0]
    qpos = i * tq + lax.broadcasted_iota(I32, (tq, 1), 0)
    u = _tri(tk, "gt")
    halves = _half_masks((tq, LANES))
    outs = []
    for hh in range(2):
        qh = jnp.where(halves[hh], q, jnp.zeros_like(q))

        def cond(c):
            kb, go, _, _ = c
            return jnp.logical_and(kb >= 0, go > 0)

        def body(c):
            kb, _, carry, acc = c
            start = pl.multiple_of(kb * tk, tk)
            z = _dot(qh, kt_ref[0, :, pl.ds(start, tk)])
            kpos = start + lax.broadcasted_iota(I32, (1, tk), 1)
            vis = kpos < qpos
            tail = _softplus_tail(z)
            lm = jnp.where(vis, -(jnp.maximum(z, 0.0) + tail), 0.0)
            local = _dot3_right(lm, u)
            w = jnp.where(vis, jnp.exp(jnp.minimum(z, 0.0) - tail + local + carry), 0.0)
            acc = acc + _dot(w.astype(BF16), v_ref[0, pl.ds(start, tk), :])
            carry = carry + local[:, 0:1] + lm[:, 0:1]
            go = (jnp.max(carry) >= SB_SKIP).astype(I32)
            return kb - 1, go, carry, acc

        init = (i, jnp.int32(1), jnp.zeros((tq, 1), F32), jnp.zeros((tq, LANES), F32))
        outs.append(lax.while_loop(cond, body, init)[3])
    o_ref[0] = jnp.where(halves[0], outs[0], outs[1]).astype(o_ref.dtype)


def _sb_attention(qb, kt, vb):
    b, t, _ = qb.shape
    tq = 256
    return pl.pallas_call(
        _sb_kernel,
        grid=(b, N_SB_PAIRS, t // tq),
        in_specs=[pl.BlockSpec((1, tq, LANES), lambda b_, p, i: (b_, i, p)),
                  pl.BlockSpec((1, LANES, t), lambda b_, p, i: (b_, p, 0)),
                  pl.BlockSpec((1, t, LANES), lambda b_, p, i: (b_, 0, p))],
        out_specs=pl.BlockSpec((1, tq, LANES), lambda b_, p, i: (b_, i, p)),
        out_shape=jax.ShapeDtypeStruct((b, t, W_HALF), BF16),
        compiler_params=_cparams(("parallel", "parallel", "parallel")),
        name="sb_attention",
    )(qb, kt, vb)


def _flash_kernel(mode, tk, *refs):
    if mode == "dsa":
        q_ref, kt_ref, v_ref, sel_ref, o_ref = refs
    else:
        q_ref, kt_ref, v_ref, fq_ref, ft_ref, o_ref = refs
    tq = q_ref.shape[1]
    p = pl.program_id(1)
    i = pl.program_id(2)
    q = q_ref[0]
    qpos = i * tq + lax.broadcasted_iota(I32, (tq, 1), 0)
    n_kb = (i * tq + tq + tk - 1) // tk
    halves = _half_masks((tq, LANES))
    outs = []
    for hh in range(2):
        qh = jnp.where(halves[hh], q, jnp.zeros_like(q))
        if mode == "fox":
            head = 2 * p + hh
            hl = lax.broadcasted_iota(I32, (tq, N_HEADS), 1)
            fq = jnp.sum(jnp.where(hl == head, fq_ref[0], 0.0), axis=1, keepdims=True)

        def body(kb, c):
            m, l, acc = c
            start = pl.multiple_of(kb * tk, tk)
            s = _dot(qh, kt_ref[0, :, pl.ds(start, tk)])
            kpos = start + lax.broadcasted_iota(I32, (1, tk), 1)
            valid = kpos <= qpos
            if mode == "dsa":
                valid = jnp.logical_and(valid, sel_ref[0, :, pl.ds(start, tk)] > 0)
            else:
                s = s + (fq - ft_ref[0, pl.ds(head, 1), pl.ds(start, tk)])
            s = jnp.where(valid, s, NEG)
            m_new = jnp.maximum(m, jnp.max(s, axis=1, keepdims=True))
            a = jnp.exp(m - m_new)
            pr = jnp.exp(s - m_new)
            l = a * l + jnp.sum(pr, axis=1, keepdims=True)
            acc = a * acc + _dot(pr.astype(BF16), v_ref[0, pl.ds(start, tk), :])
            return m_new, l, acc

        init = (jnp.full((tq, 1), NEG, F32), jnp.zeros((tq, 1), F32), jnp.zeros((tq, LANES), F32))
        _, l, acc = lax.fori_loop(0, n_kb, body, init)
        outs.append(acc / l)
    o_ref[0] = jnp.where(halves[0], outs[0], outs[1]).astype(o_ref.dtype)


def _flash_attention(mode, qb, kt, vb, extra, pair0, n_pairs):
    b, t, _ = qb.shape
    tq = 256
    tk = min(512, t)
    in_specs = [pl.BlockSpec((1, tq, LANES), lambda b_, p, i: (b_, i, pair0 + p)),
                pl.BlockSpec((1, LANES, t), lambda b_, p, i: (b_, pair0 + p, 0)),
                pl.BlockSpec((1, t, LANES), lambda b_, p, i: (b_, 0, pair0 + p))]
    if mode == "dsa":
        in_specs.append(pl.BlockSpec((1, tq, t), lambda b_, p, i: (b_, i, 0)))
    else:
        in_specs += [pl.BlockSpec((1, tq, N_HEADS), lambda b_, p, i: (b_, i, 0)),
                     pl.BlockSpec((1, N_HEADS, t), lambda b_, p, i: (b_, 0, 0))]
    return pl.pallas_call(
        functools.partial(_flash_kernel, mode, tk),
        grid=(b, n_pairs, t // tq),
        in_specs=in_specs,
        out_specs=pl.BlockSpec((1, tq, LANES), lambda b_, p, i: (b_, i, p)),
        out_shape=jax.ShapeDtypeStruct((b, t, n_pairs * LANES), BF16),
        compiler_params=_cparams(("parallel", "parallel", "parallel")),
        name=mode + "_attention",
    )(qb, kt, vb, *extra)


def _idx_scores(qix, wix, kix):
    score = None
    for hi in range(N_IDX_HEADS):
        s = _dot_nt(qix[:, hi * IDX_DIM:(hi + 1) * IDX_DIM], kix)
        term = wix[:, hi:hi + 1] * jnp.maximum(s, 0.0)
        score = term if score is None else score + term
    return score


def _sort_key(score):
    bits = lax.bitcast_convert_type(score, I32)
    key = jnp.where(bits < 0, bits ^ jnp.int32(0x7FFFFFFF), bits)
    return jnp.where(score == 0.0, 0, key)


def _select_topk(keys_ref, out_ref, n_chunks, top):
    rows, width = keys_ref.shape
    total_chunks = width // LANES

    def count(pred, cand):
        def body(c, acc):
            kc = keys_ref[:, pl.ds(pl.multiple_of(c * LANES, LANES), LANES)]
            return acc + jnp.where(pred(kc, cand), 1, 0)
        acc = lax.fori_loop(0, n_chunks, body, jnp.zeros((rows, LANES), I32))
        return jnp.sum(acc, axis=1, keepdims=True)

    ge = lambda kc, cand: kc >= cand
    zero = jnp.zeros((rows, 1), I32)
    thr = jnp.where(count(ge, zero) >= top, zero, jnp.full((rows, 1), INT_MIN, I32))

    def bit_body(bi, thr):
        cand = thr | (jnp.int32(1) << (30 - bi))
        return jnp.where(count(ge, cand) >= top, cand, thr)

    thr = lax.fori_loop(0, 31, bit_body, thr)
    need = (top - count(lambda kc, cand: kc > cand, thr)).astype(F32)
    u = _tri(LANES, "lt")

    def sel_body(c, seen):
        off = pl.multiple_of(c * LANES, LANES)
        kc = keys_ref[:, pl.ds(off, LANES)]
        eq = jnp.where(kc == thr, 1.0, 0.0)
        before = _dot(eq.astype(BF16), u) + seen
        sel = jnp.logical_or(kc > thr, jnp.logical_and(kc == thr, before < need))
        out_ref[:, pl.ds(off, LANES)] = jnp.where(sel, 1.0, 0.0).astype(out_ref.dtype)
        return seen + jnp.sum(eq, axis=1, keepdims=True)

    lax.fori_loop(0, n_chunks, sel_body, jnp.zeros((rows, 1), F32))

    def zero_body(c, _):
        out_ref[:, pl.ds(pl.multiple_of(c * LANES, LANES), LANES)] = jnp.zeros((rows, LANES), out_ref.dtype)
        return 0

    lax.fori_loop(n_chunks, total_chunks, zero_body, 0)


def _prompt_select_kernel(top, ck, qix_ref, wix_ref, kix_ref, sel_ref, keys_sc):
    tq = qix_ref.shape[1]
    i = pl.program_id(1)
    qix = qix_ref[0]
    wix = wix_ref[0]
    qpos = i * tq + lax.broadcasted_iota(I32, (tq, 1), 0)
    n_blocks = (i * tq + tq + ck - 1) // ck

    def score_body(c, _):
        start = pl.multiple_of(c * ck, ck)
        score = _idx_scores(qix, wix, kix_ref[0, pl.ds(start, ck), :])
        kpos = start + lax.broadcasted_iota(I32, (1, ck), 1)
        score = jnp.where(kpos <= qpos, score, -jnp.inf)
        keys_sc[:, pl.ds(start, ck)] = _sort_key(score)
        return 0

    lax.fori_loop(0, n_blocks, score_body, 0)
    _select_topk(keys_sc, sel_ref.at[0], n_blocks * (ck // LANES), top)


def _prompt_select(qix, wix, kixb):
    b, t, _ = qix.shape
    tq = 256
    ck = min(512, t)
    top = min(DSA_TOPK, t // 4)
    return pl.pallas_call(
        functools.partial(_prompt_select_kernel, top, ck),
        grid=(b, t // tq),
        in_specs=[pl.BlockSpec((1, tq, N_IDX_HEADS * IDX_DIM), lambda b_, i: (b_, i, 0)),
                  pl.BlockSpec((1, tq, N_IDX_HEADS), lambda b_, i: (b_, i, 0)),
                  pl.BlockSpec((1, t, IDX_DIM), lambda b_, i: (b_, 0, 0))],
        out_specs=pl.BlockSpec((1, tq, t), lambda b_, i: (b_, i, 0)),
        out_shape=jax.ShapeDtypeStruct((b, t, t), BF16),
        scratch_shapes=[pltpu.VMEM((tq, t), I32)],
        compiler_params=_cparams(("parallel", "parallel")),
        name="prompt_select",
    )(qix, wix, kixb)


def _dec_scores_kernel(n_pages, past, pt_ref, qix_ref, wix_ref, kc_ref, kn_ref, s_ref):
    j = pl.program_id(1)
    rows = qix_ref.shape[1]
    qpos = past + lax.broadcasted_iota(I32, (rows, 1), 0)
    kpos = j * PAGE + lax.broadcasted_iota(I32, (1, PAGE), 1)

    def emit(kix):
        score = _idx_scores(qix_ref[0], wix_ref[0], kix.astype(BF16))
        s_ref[0] = jnp.where(kpos <= qpos, score, -jnp.inf)

    @pl.when(j < n_pages)
    def _():
        emit(kc_ref[...])

    @pl.when(j == n_pages)
    def _():
        emit(kn_ref[0])


def _dec_scores(page_table, qix, wix, kix_cache, kix_new, past):
    db, n_pages = page_table.shape
    rows = qix.shape[1]
    grid_spec = pltpu.PrefetchScalarGridSpec(
        num_scalar_prefetch=1,
        grid=(db, n_pages + 1),
        in_specs=[pl.BlockSpec((1, rows, N_IDX_HEADS * IDX_DIM), lambda b, j, pt: (b, 0, 0)),
                  pl.BlockSpec((1, rows, N_IDX_HEADS), lambda b, j, pt: (b, 0, 0)),
                  pl.BlockSpec((None, PAGE, IDX_DIM),
                               lambda b, j, pt: (pt[b, jnp.minimum(j, n_pages - 1)], 0, 0)),
                  pl.BlockSpec((1, PAGE, IDX_DIM), lambda b, j, pt: (b, 0, 0))],
        out_specs=pl.BlockSpec((1, rows, PAGE), lambda b, j, pt: (b, 0, j)))
    return pl.pallas_call(
        functools.partial(_dec_scores_kernel, n_pages, past),
        grid_spec=grid_spec,
        out_shape=jax.ShapeDtypeStruct((db, rows, (n_pages + 1) * PAGE), F32),
        compiler_params=_cparams(("parallel", "arbitrary")),
        name="dec_scores",
    )(page_table, qix, wix, kix_cache, kix_new)


def _dec_select_kernel(top, s_ref, sel_ref, keys_sc):
    keys_sc[...] = _sort_key(s_ref[...])
    _select_topk(keys_sc, sel_ref, s_ref.shape[1] // LANES, top)


def _dec_select(scores, top):
    r, width = scores.shape
    tr = min(256, r)
    return pl.pallas_call(
        functools.partial(_dec_select_kernel, top),
        grid=(r // tr,),
        in_specs=[pl.BlockSpec((tr, width), lambda i: (i, 0))],
        out_specs=pl.BlockSpec((tr, width), lambda i: (i, 0)),
        out_shape=jax.ShapeDtypeStruct((r, width), BF16),
        scratch_shapes=[pltpu.VMEM((tr, width), I32)],
        compiler_params=_cparams(("parallel",)),
        name="dec_select",
    )(scores)


def _dec_suffix_kernel(n_pages, pt_ref, lc_ref, ln_ref, s_ref, carry_sc):
    j = pl.program_id(1)

    @pl.when(j == 0)
    def _():
        carry_sc[...] = jnp.zeros_like(carry_sc)

    def emit(lf):
        s_ref[0, 0] = _dot3_left(_tri(PAGE, "lt"), lf) + carry_sc[...]
        carry_sc[...] += jnp.sum(lf, axis=0, keepdims=True)

    @pl.when(j == 0)
    def _():
        emit(ln_ref[0])

    @pl.when(j > 0)
    def _():
        emit(lc_ref[...])


def _dec_suffix(page_table, logf_cache, logf_new):
    db, n_pages = page_table.shape
    grid_spec = pltpu.PrefetchScalarGridSpec(
        num_scalar_prefetch=1,
        grid=(db, n_pages + 1),
        in_specs=[pl.BlockSpec((None, PAGE, N_HEADS),
                               lambda b, j, pt: (pt[b, n_pages - jnp.maximum(j, 1)], 0, 0)),
                  pl.BlockSpec((1, PAGE, N_HEADS), lambda b, j, pt: (b, 0, 0))],
        out_specs=pl.BlockSpec((1, 1, PAGE, N_HEADS), lambda b, j, pt: (b, j, 0, 0)),
        scratch_shapes=[pltpu.VMEM((1, N_HEADS), F32)])
    return pl.pallas_call(
        functools.partial(_dec_suffix_kernel, n_pages),
        grid_spec=grid_spec,
        out_shape=jax.ShapeDtypeStruct((db, n_pages + 1, PAGE, N_HEADS), F32),
        compiler_params=_cparams(("parallel", "arbitrary")),
        name="dec_suffix",
    )(page_table, logf_cache, logf_new)


def _dec_attn_kernel(mode, n_pages, past, n_q, pt_ref, *refs):
    if mode == "even":
        q_ref, kc_ref, vc_ref, kn_ref, vn_ref, sel_ref, ex_ref, o_ref, m_sc, l_sc, acc_sc, carry_sc = refs
    else:
        q_ref, kc_ref, vc_ref, kn_ref, vn_ref, sk_ref, sq_ref, o_ref, m_sc, l_sc, acc_sc = refs
    j = pl.program_id(1)
    n_rows = q_ref.shape[1]
    n_lanes = PAGE * N_HEADS
    n_sb = N_SB_HEADS * n_q if mode == "even" else 0

    @pl.when(j == 0)
    def _():
        m_sc[...] = jnp.full_like(m_sc, NEG)
        l_sc[...] = jnp.zeros_like(l_sc)
        acc_sc[...] = jnp.zeros_like(acc_sc)
        if mode == "even":
            carry_sc[...] = jnp.zeros_like(carry_sc)

    logical_page = jnp.where(j == 0, n_pages, n_pages - j)
    lane = lax.broadcasted_iota(I32, (1, n_lanes), 1)
    kpos = logical_page * PAGE + lane // N_HEADS
    row = lax.broadcasted_iota(I32, (n_rows, 1), 0)
    qpos = past + row % n_q
    own = (lane % N_HEADS) == (row // n_q)

    def softmax_rows(r0, r1, s, valid, v2):
        s = jnp.where(valid, s, NEG)
        m_old = m_sc[r0:r1]
        m_new = jnp.maximum(m_old, jnp.max(s, axis=1, keepdims=True))
        a = jnp.exp(m_old - m_new)
        pr = jnp.where(valid, jnp.exp(s - m_new), 0.0)
        l_sc[r0:r1] = a * l_sc[r0:r1] + jnp.sum(pr, axis=1, keepdims=True)
        acc_sc[r0:r1] = a * acc_sc[r0:r1] + _dot(pr.astype(BF16), v2)
        m_sc[r0:r1] = m_new

    def step(k2, v2):
        k2 = k2.astype(BF16)
        v2 = v2.astype(BF16)
        s = _dot_nt(q_ref[0], k2)
        if mode == "even":
            z = s[0:n_sb]
            vis = jnp.logical_and(own[0:n_sb], kpos < qpos[0:n_sb])
            tail = _softplus_tail(z)
            lm = jnp.where(vis, -(jnp.maximum(z, 0.0) + tail), 0.0)
            u = _tri(LANES, "gt")
            after = jnp.zeros((n_sb, 1), F32)
            pieces = []
            for c in reversed(range(n_lanes // LANES)):
                lmc = lm[:, c * LANES:(c + 1) * LANES]
                local = _dot3_right(lmc, u)
                pieces.append(local + after)
                after = after + local[:, 0:1] + lmc[:, 0:1]
            rest = jnp.concatenate(pieces[::-1], axis=1) + carry_sc[...]
            w = jnp.where(vis, jnp.exp(jnp.minimum(z, 0.0) - tail + rest), 0.0)
            acc_sc[0:n_sb] += _dot(w.astype(BF16), v2)
            carry_sc[...] += after
            selx = _dot(sel_ref[0], ex_ref[...]) > 0.5
            valid = jnp.logical_and(jnp.logical_and(own[n_sb:], kpos <= qpos[n_sb:]), selx)
            softmax_rows(n_sb, n_rows, s[n_sb:], valid, v2)
        else:
            valid = jnp.logical_and(own, kpos <= qpos)
            softmax_rows(0, n_rows, s + (sk_ref[0, 0] - sq_ref[0]), valid, v2)

    @pl.when(j == 0)
    def _():
        step(kn_ref[0], vn_ref[0])

    @pl.when(j > 0)
    def _():
        step(kc_ref[...], vc_ref[...])

    @pl.when(j == n_pages)
    def _():
        if mode == "even":
            o_ref[0, 0:n_sb] = acc_sc[0:n_sb]
            o_ref[0, n_sb:] = acc_sc[n_sb:] / l_sc[n_sb:]
        else:
            o_ref[0] = acc_sc[...] / l_sc[...]


def _dec_attention(mode, page_table, qmat, k_cache, v_cache, k_new, v_new, extra, past, n_q):
    db, n_pages = page_table.shape
    n_rows = qmat.shape[1]
    n_lanes = PAGE * N_HEADS
    page_idx = lambda b, j, pt: (pt[b, n_pages - jnp.maximum(j, 1)], 0, 0)
    per_b = lambda b, j, pt: (b, 0, 0)
    col_blk = lambda b, j, pt: (b, 0, jnp.where(j == 0, n_pages, n_pages - j))
    in_specs = [pl.BlockSpec((1, n_rows, HEAD_DIM), per_b),
                pl.BlockSpec((None, n_lanes, HEAD_DIM), page_idx),
                pl.BlockSpec((None, n_lanes, HEAD_DIM), page_idx),
                pl.BlockSpec((1, n_lanes, HEAD_DIM), per_b),
                pl.BlockSpec((1, n_lanes, HEAD_DIM), per_b)]
    scratch = [pltpu.VMEM((n_rows, 1), F32), pltpu.VMEM((n_rows, 1), F32),
               pltpu.VMEM((n_rows, HEAD_DIM), F32)]
    if mode == "even":
        sel, expand = extra
        n_ds = sel.shape[1]
        in_specs += [pl.BlockSpec((1, n_ds, PAGE), col_blk),
                     pl.BlockSpec((PAGE, n_lanes), lambda b, j, pt: (0, 0))]
        scratch.append(pltpu.VMEM((n_rows - n_ds, 1), F32))
    else:
        in_specs += [pl.BlockSpec((1, 1, 1, n_lanes), lambda b, j, pt: (b, j, 0, 0)),
                     pl.BlockSpec((1, n_rows, 1), per_b)]
    grid_spec = pltpu.PrefetchScalarGridSpec(
        num_scalar_prefetch=1,
        grid=(db, n_pages + 1),
        in_specs=in_specs,
        out_specs=pl.BlockSpec((1, n_rows, HEAD_DIM), per_b),
        scratch_shapes=scratch)
    return pl.pallas_call(
        functools.partial(_dec_attn_kernel, mode, n_pages, past, n_q),
        grid_spec=grid_spec,
        out_shape=jax.ShapeDtypeStruct((db, n_rows, HEAD_DIM), F32),
        compiler_params=_cparams(("parallel", "arbitrary")),
        name="dec_attention_" + mode,
    )(page_table, qmat, k_cache, v_cache, k_new, v_new, *extra)


def _rope_tables(pos):
    half = HEAD_DIM // 2
    inv = ROPE_THETA ** (-jnp.arange(half, dtype=F32) / half)
    ang = pos.astype(F32)[:, None] * inv
    cos, sin = jnp.cos(ang), jnp.sin(ang)
    reps = W_HALF // HEAD_DIM
    return (jnp.tile(jnp.concatenate([cos, cos], axis=-1), (1, reps)),
            jnp.tile(jnp.concatenate([-sin, sin], axis=-1), (1, reps)))


def _pad_new(a, db, n_q):
    a = a.reshape(db, n_q, *a.shape[1:])
    pad = [(0, 0), (0, PAGE - n_q)] + [(0, 0)] * (a.ndim - 2)
    return jnp.pad(a, pad)


def _rows_head_query(a, db, n_q):
    return a.reshape(db, n_q, N_HEADS, HEAD_DIM).transpose(0, 2, 1, 3).reshape(db, N_HEADS * n_q, HEAD_DIM)


def _rows_token(o, db, n_q):
    return o.reshape(db, N_HEADS, n_q, HEAD_DIM).transpose(0, 2, 1, 3).reshape(1, db * n_q, D_MODEL)


def kernel(x_prompt, x_sample, cache_k_even, cache_v_even, cache_kidx_even, cache_k_odd, cache_v_odd,
           cache_logf_odd, page_table, c_prompt, c_sample, w_ada, b_ada, ln_g, ln_b, ffn_w1, ffn_w3,
           ffn_w2, w_in_even, w_out_even, w_in_odd, b_forget, w_out_odd):
    b, t, _ = x_prompt.shape
    db, n_q, _ = x_sample.shape
    n_pages = page_table.shape[1]
    past = n_pages * PAGE
    pool = cache_k_even.shape[1]

    n_c = b + db
    c_all = jnp.pad(jnp.concatenate([c_prompt, c_sample], axis=0), ((0, (-n_c) % 8), (0, 0)))
    mod = _ada_mod(c_all, w_ada, b_ada).reshape(DEPTH, -1, N_SUB, 3, D_MODEL)

    def mods(layer, sub):
        m = mod[layer, :, sub]
        mp = [m[:b, i].reshape(b, 1, D_MODEL) for i in range(3)]
        ms = [jnp.repeat(m[b:n_c, i], n_q, axis=0).reshape(1, db * n_q, D_MODEL) for i in range(3)]
        return mp, ms

    cos_p, sin_p = _rope_tables(jnp.tile(jnp.arange(t, dtype=I32), b))
    cos_s, sin_s = _rope_tables(jnp.tile(past + jnp.arange(n_q, dtype=I32), db))

    w_even = jnp.pad(w_in_even, ((0, 0), (0, 0), (0, P_EVEN_PAD - P_EVEN))).astype(BF16)
    w_odd = jnp.pad(w_in_odd, ((0, 0), (0, 0), (0, P_ODD_PAD - w_in_odd.shape[-1]))).astype(BF16)
    bf_pad = jnp.pad(b_forget, ((0, 0), (0, LANES - N_HEADS)))
    w1, w3, w2 = ffn_w1.astype(BF16), ffn_w3.astype(BF16), ffn_w2.astype(BF16)
    wo_even, wo_odd = w_out_even.astype(BF16), w_out_odd.astype(BF16)

    lane = jnp.arange(PAGE * N_HEADS, dtype=I32)
    expand = (lane[None, :] // N_HEADS == jnp.arange(PAGE, dtype=I32)[:, None]).astype(BF16)
    top_s = min(DSA_TOPK, (past + n_q) // 4)

    xp = x_prompt
    xs = x_sample.reshape(1, db * n_q, D_MODEL)
    even_p, odd_p, even_s, odd_s = [], [], [], []
    for layer in range(DEPTH):
        j = layer // 2
        (mp0, ms0), (mp1, ms1), (mp2, ms2) = mods(layer, 0), mods(layer, 1), mods(layer, 2)
        ffn_a = (w1[layer, 0], w3[layer, 0], w2[layer, 0], ln_g[layer, 0], ln_b[layer, 0])
        xp = _ffn_sublayer(xp, *mp0, *ffn_a)
        xs = _ffn_sublayer(xs, *ms0, *ffn_a)
        if layer % 2 == 0:
            k32, v32, kix32, qb, kt, vb, qix, wix, kixb = _even_proj(xp, mp1[0], mp1[1], w_even[j], cos_p, sin_p)
            even_p.append((k32.reshape(b, t, N_HEADS, HEAD_DIM), v32.reshape(b, t, N_HEADS, HEAD_DIM), kix32))
            o_sb = _sb_attention(qb, kt, vb)
            sel = _prompt_select(qix, wix, kixb)
            o_ds = _flash_attention("dsa", qb, kt, vb, (sel,), N_SB_PAIRS, N_PAIRS - N_SB_PAIRS)
            xp = _out_proj((o_sb, o_ds), (wo_even[j, :W_HALF], wo_even[j, W_HALF:]), xp, mp1[2],
                           ln_g[layer, 1], ln_b[layer, 1])
            k32, v32, kix32, qb, _, _, qix, wix, _ = _even_proj(xs, ms1[0], ms1[1], w_even[j], cos_s, sin_s)
            even_s.append((k32.reshape(db, n_q, N_HEADS, HEAD_DIM), v32.reshape(db, n_q, N_HEADS, HEAD_DIM),
                           kix32.reshape(db, n_q, IDX_DIM)))
            pad_q = lambda a: jnp.pad(a.reshape(db, n_q, -1), ((0, 0), (0, 8 - n_q), (0, 0)))
            scores = _dec_scores(page_table, pad_q(qix), pad_q(wix), cache_kidx_even[j],
                                 _pad_new(kix32[0], db, n_q), past)
            sel = _dec_select(scores.reshape(db * 8, -1), top_s).reshape(db, 8, -1)[:, :n_q]
            sel = jnp.broadcast_to(sel[:, None], (db, N_HEADS - N_SB_HEADS, n_q, sel.shape[-1]))
            sel = sel.reshape(db, (N_HEADS - N_SB_HEADS) * n_q, -1)
            o = _dec_attention(
                "even", page_table, _rows_head_query(qb[0], db, n_q),
                cache_k_even[j].reshape(pool, PAGE * N_HEADS, HEAD_DIM),
                cache_v_even[j].reshape(pool, PAGE * N_HEADS, HEAD_DIM),
                _pad_new(k32[0], db, n_q).reshape(db, PAGE * N_HEADS, HEAD_DIM),
                _pad_new(v32[0], db, n_q).reshape(db, PAGE * N_HEADS, HEAD_DIM),
                (sel, expand), past, n_q)
            xs = _out_proj((_rows_token(o, db, n_q).astype(BF16),), (wo_even[j],), xs, ms1[2],
                           ln_g[layer, 1], ln_b[layer, 1])
        else:
            k32, v32, logf, qb, kt, vb, fq, ft = _odd_proj(xp, mp1[0], mp1[1], w_odd[j], bf_pad[j:j + 1])
            odd_p.append((k32.reshape(b, t, N_HEADS, HEAD_DIM), v32.reshape(b, t, N_HEADS, HEAD_DIM), logf))
            o = _flash_attention("fox", qb, kt, vb, (fq, ft), 0, N_PAIRS)
            xp = _out_proj((o,), (wo_odd[j],), xp, mp1[2], ln_g[layer, 1], ln_b[layer, 1])
            k32, v32, logf, qb, _, _, _, _ = _odd_proj(xs, ms1[0], ms1[1], w_odd[j], bf_pad[j:j + 1])
            odd_s.append((k32.reshape(db, n_q, N_HEADS, HEAD_DIM), v32.reshape(db, n_q, N_HEADS, HEAD_DIM),
                          logf.reshape(db, n_q, N_HEADS)))
            suffix = _dec_suffix(page_table, cache_logf_odd[j], _pad_new(logf[0], db, n_q))
            s_keys = suffix.reshape(db, n_pages + 1, 1, PAGE * N_HEADS)
            s_q = suffix[:, 0, :n_q, :].transpose(0, 2, 1).reshape(db, N_HEADS * n_q, 1)
            o = _dec_attention(
                "odd", page_table, _rows_head_query(qb[0], db, n_q),
                cache_k_odd[j].reshape(pool, PAGE * N_HEADS, HEAD_DIM),
                cache_v_odd[j].reshape(pool, PAGE * N_HEADS, HEAD_DIM),
                _pad_new(k32[0], db, n_q).reshape(db, PAGE * N_HEADS, HEAD_DIM),
                _pad_new(v32[0], db, n_q).reshape(db, PAGE * N_HEADS, HEAD_DIM),
                (s_keys, s_q), past, n_q)
            xs = _out_proj((_rows_token(o, db, n_q).astype(BF16),), (wo_odd[j],), xs, ms1[2],
                           ln_g[layer, 1], ln_b[layer, 1])
        ffn_b = (w1[layer, 1], w3[layer, 1], w2[layer, 1], ln_g[layer, 2], ln_b[layer, 2])
        xp = _ffn_sublayer(xp, *mp2, *ffn_b)
        xs = _ffn_sublayer(xs, *ms2, *ffn_b)

    def stack(rows, i):
        return jnp.stack([r[i] for r in rows])

    return (xp, xs.reshape(db, n_q, D_MODEL),
            stack(even_p, 0), stack(even_p, 1), stack(even_p, 2),
            stack(odd_p, 0), stack(odd_p, 1), stack(odd_p, 2),
            stack(even_s, 0), stack(even_s, 1), stack(even_s, 2),
            stack(odd_s, 0), stack(odd_s, 1), stack(odd_s, 2))
```

```python
import functools
import math

import jax
import jax.numpy as jnp
from jax import lax
from jax.experimental import pallas as pl
from jax.experimental.pallas import tpu as pltpu

F32 = jnp.float32
BF16 = jnp.bfloat16
I32 = jnp.int32

D_MODEL = 1024
HEAD_DIM = 64
N_HEADS = 16
N_SB_HEADS = 8
N_PAIRS = N_HEADS // 2
N_SB_PAIRS = N_SB_HEADS // 2
W_HALF = N_SB_HEADS * HEAD_DIM
N_IDX_HEADS = 4
IDX_DIM = 64
DSA_TOPK = 256
D_FF = 2816
ROPE_THETA = 10000.0
PAGE = 128
N_SUB = 3
DEPTH = 2
ALPHA = (2 * DEPTH) ** 0.25
LN_EPS = 1e-5
LANES = 128
P_EVEN = 6 * W_HALF + N_IDX_HEADS * IDX_DIM + IDX_DIM + N_IDX_HEADS
P_EVEN_PAD = 3456
IDX_COL = 6 * W_HALF
IDX_W = P_EVEN_PAD - IDX_COL
P_ODD_PAD = 3 * D_MODEL + LANES
NEG = -1e30
SB_SKIP = -105.0
INT_MIN = -(2 ** 31)
LOG2E = math.log2(math.e)
SEL_CHUNK = 512
DEC_ROWS = 128
DEC_QPAD = 16
DEC_PAGES = 4
DEC_SCORE_PAGES = 8
VMEM_LIMIT = 56 * 1024 * 1024


def _cparams(sem):
    return pltpu.CompilerParams(dimension_semantics=sem, vmem_limit_bytes=VMEM_LIMIT)


def _dot(a, b):
    return jnp.dot(a, b, preferred_element_type=F32)


def _dot_nt(a, b):
    return lax.dot_general(a, b, (((1,), (1,)), ((), ())), preferred_element_type=F32)


def _split3(x):
    hi = x.astype(BF16)
    r = x - hi.astype(F32)
    mid = r.astype(BF16)
    lo = (r - mid.astype(F32)).astype(BF16)
    return hi, mid, lo


def _dot3_right(x, u):
    hi, mid, lo = _split3(x)
    return _dot(hi, u) + _dot(mid, u) + _dot(lo, u)


def _dot3_left(u, x):
    hi, mid, lo = _split3(x)
    return _dot(u, hi) + _dot(u, mid) + _dot(u, lo)


def _tri(n, rel):
    r = lax.broadcasted_iota(I32, (n, n), 0)
    c = lax.broadcasted_iota(I32, (n, n), 1)
    m = {"gt": r > c, "lt": r < c, "ge": r >= c, "le": r <= c}[rel]
    return jnp.where(m, 1.0, 0.0).astype(BF16)


def _layer_norm(y, g, b):
    mu = jnp.mean(y, axis=-1, keepdims=True)
    d = y - mu
    var = jnp.mean(d * d, axis=-1, keepdims=True)
    return d * lax.rsqrt(var + LN_EPS) * g + b


def _softplus_tail(z):
    return jnp.log(1.0 + jnp.exp(-jnp.abs(z)))


def _suffix_sums(x, carry):
    u = _tri(LANES, "gt")
    after = carry
    pieces = []
    for c in reversed(range(x.shape[1] // LANES)):
        xc = x[:, c * LANES:(c + 1) * LANES]
        local = _dot3_right(xc, u)
        pieces.append(local + after)
        after = after + local[:, 0:1] + xc[:, 0:1]
    out = pieces[0] if len(pieces) == 1 else jnp.concatenate(pieces[::-1], axis=1)
    return out, after


def _col_to_row(col):
    return jnp.broadcast_to(col, (LANES, LANES)).T[0:1, :]


def _mod_spec(arr, tm, n_grid):
    if arr.shape[1] == 1:
        if n_grid == 2:
            return pl.BlockSpec((1, 1, D_MODEL), lambda g, t: (g, 0, 0))
        return pl.BlockSpec((1, 1, D_MODEL), lambda g, t, k: (g, 0, 0))
    if n_grid == 2:
        return pl.BlockSpec((1, tm, D_MODEL), lambda g, t: (g, t, 0))
    return pl.BlockSpec((1, tm, D_MODEL), lambda g, t, k: (g, t, 0))


def _ada_kernel(c_ref, w_ref, b_ref, o_ref):
    c = c_ref[...]
    a = (c * jax.nn.sigmoid(c)).astype(BF16)
    o_ref[0] = _dot(a, w_ref[0].astype(BF16)) + b_ref[0]


def _ada_mod(c_all, w_ada, b_ada):
    m = c_all.shape[0]
    depth, _, n = w_ada.shape
    tn = 1152
    return pl.pallas_call(
        _ada_kernel,
        grid=(depth, n // tn),
        in_specs=[pl.BlockSpec((m, D_MODEL), lambda l, j: (0, 0)),
                  pl.BlockSpec((1, D_MODEL, tn), lambda l, j: (l, 0, j)),
                  pl.BlockSpec((1, 1, tn), lambda l, j: (l, 0, j))],
        out_specs=pl.BlockSpec((1, m, tn), lambda l, j: (l, 0, j)),
        out_shape=jax.ShapeDtypeStruct((depth, m, n), F32),
        compiler_params=_cparams(("parallel", "parallel")),
        name="ada_mod",
    )(c_all, w_ada, b_ada.reshape(depth, 1, n))


def _ffn_kernel(x_ref, sh_ref, sc_ref, gt_ref, w1_ref, w3_ref, w2_ref, g_ref, b_ref, o_ref,
                h_sc, acc_sc):
    k = pl.program_id(2)

    @pl.when(k == 0)
    def _():
        h_sc[...] = (x_ref[0] * (1.0 + sc_ref[0]) + sh_ref[0]).astype(BF16)
        acc_sc[...] = jnp.zeros_like(acc_sc)

    h = h_sc[...]
    a = _dot(h, w1_ref[...])
    b = _dot(h, w3_ref[...])
    act = (a * jax.nn.sigmoid(a) * b).astype(BF16)
    acc_sc[...] += _dot(act, w2_ref[...])

    @pl.when(k == pl.num_programs(2) - 1)
    def _():
        y = ALPHA * x_ref[0] + gt_ref[0] * (0.5 * acc_sc[...])
        o_ref[0] = _layer_norm(y, g_ref[...], b_ref[...])


def _ffn_sublayer(x, shift, scale, gate, w1, w3, w2, ln_g, ln_b):
    g, tg, _ = x.shape
    tm = min(512, tg)
    tf = D_FF // 2
    xs = pl.BlockSpec((1, tm, D_MODEL), lambda g_, t, k: (g_, t, 0))
    vec = pl.BlockSpec((1, D_MODEL), lambda g_, t, k: (0, 0))
    return pl.pallas_call(
        _ffn_kernel,
        grid=(g, tg // tm, D_FF // tf),
        in_specs=[xs, _mod_spec(shift, tm, 3), _mod_spec(scale, tm, 3), _mod_spec(gate, tm, 3),
                  pl.BlockSpec((D_MODEL, tf), lambda g_, t, k: (0, k)),
                  pl.BlockSpec((D_MODEL, tf), lambda g_, t, k: (0, k)),
                  pl.BlockSpec((tf, D_MODEL), lambda g_, t, k: (k, 0)),
                  vec, vec],
        out_specs=xs,
        out_shape=jax.ShapeDtypeStruct(x.shape, F32),
        scratch_shapes=[pltpu.VMEM((tm, D_MODEL), BF16), pltpu.VMEM((tm, D_MODEL), F32)],
        compiler_params=_cparams(("parallel", "parallel", "arbitrary")),
        name="ffn_sublayer",
    )(x, shift, scale, gate, w1, w3, w2, ln_g.reshape(1, D_MODEL), ln_b.reshape(1, D_MODEL))


def _rotate(x, cos, sin_signed):
    n = x.shape[-1]
    lane = lax.broadcasted_iota(I32, x.shape, 1)
    first = (lane % HEAD_DIM) < (HEAD_DIM // 2)
    other = jnp.where(first, pltpu.roll(x, n - HEAD_DIM // 2, 1), pltpu.roll(x, HEAD_DIM // 2, 1))
    return x * cos[:, :n] + other * sin_signed[:, :n]


def _even_proj_kernel(x_ref, sh_ref, sc_ref, w_ref, cos_ref, sin_ref,
                      kt_ref, vt_ref, kixt_ref, qb_ref, ktb_ref, vb_ref, qix_ref, wix_ref, kixtb_ref):
    h = (x_ref[0] * (1.0 + sc_ref[0]) + sh_ref[0]).astype(BF16)
    cos = cos_ref[...]
    sin = sin_ref[...]

    def proj(lo, width):
        return _dot(h, w_ref[:, lo:lo + width])

    scale = HEAD_DIM ** -0.5
    q_sb = proj(0, W_HALF)
    qb_ref[0, :, 0:W_HALF] = (q_sb * scale).astype(BF16)
    k_sb = proj(W_HALF, W_HALF).T
    kt_ref[0, 0:W_HALF, :] = k_sb
    ktb_ref[0, 0:W_HALF, :] = k_sb.astype(BF16)
    v_sb = proj(2 * W_HALF, W_HALF)
    vt_ref[0, 0:W_HALF, :] = v_sb.T
    vb_ref[0, :, 0:W_HALF] = v_sb.astype(BF16)
    q_ds = _rotate(proj(3 * W_HALF, W_HALF), cos, sin)
    qb_ref[0, :, W_HALF:2 * W_HALF] = (q_ds * (scale * LOG2E)).astype(BF16)
    k_ds = _rotate(proj(4 * W_HALF, W_HALF), cos, sin).T
    kt_ref[0, W_HALF:2 * W_HALF, :] = k_ds
    ktb_ref[0, W_HALF:2 * W_HALF, :] = k_ds.astype(BF16)
    v_ds = proj(5 * W_HALF, W_HALF)
    vt_ref[0, W_HALF:2 * W_HALF, :] = v_ds.T
    vb_ref[0, :, W_HALF:2 * W_HALF] = v_ds.astype(BF16)
    ix = proj(IDX_COL, IDX_W)
    ixr = _rotate(ix, cos, sin)
    nq = N_IDX_HEADS * IDX_DIM
    qix_ref[0] = (ixr[:, 0:nq] * IDX_DIM ** -0.5).astype(BF16)
    kix = ixr[:, nq:nq + LANES].T[0:IDX_DIM, :]
    kixt_ref[0] = kix
    kixtb_ref[0] = kix.astype(BF16)
    wix_ref[0] = ix[:, nq + IDX_DIM:nq + IDX_DIM + N_IDX_HEADS] * N_IDX_HEADS ** -0.5


def _even_proj(x, shift, scale, w_pad, cos, sin):
    g, tg, _ = x.shape
    tm = min(512, tg)
    nt = tg // tm
    row = lambda width: pl.BlockSpec((1, tm, width), lambda g_, t: (g_, t, 0))
    col = lambda height: pl.BlockSpec((1, height, tm), lambda g_, t: (g_, 0, t))
    tab = pl.BlockSpec((tm, W_HALF), lambda g_, t: (g_ * nt + t, 0))
    rows = lambda width, dt: jax.ShapeDtypeStruct((g, tg, width), dt)
    cols = lambda height, dt: jax.ShapeDtypeStruct((g, height, tg), dt)
    return pl.pallas_call(
        _even_proj_kernel,
        grid=(g, nt),
        in_specs=[row(D_MODEL), _mod_spec(shift, tm, 2), _mod_spec(scale, tm, 2),
                  pl.BlockSpec((D_MODEL, P_EVEN_PAD), lambda g_, t: (0, 0)), tab, tab],
        out_specs=[col(D_MODEL), col(D_MODEL), col(IDX_DIM), row(D_MODEL), col(D_MODEL), row(D_MODEL),
                   row(N_IDX_HEADS * IDX_DIM), row(N_IDX_HEADS), col(IDX_DIM)],
        out_shape=[cols(D_MODEL, F32), cols(D_MODEL, F32), cols(IDX_DIM, F32), rows(D_MODEL, BF16),
                   cols(D_MODEL, BF16), rows(D_MODEL, BF16), rows(N_IDX_HEADS * IDX_DIM, BF16),
                   rows(N_IDX_HEADS, F32), cols(IDX_DIM, BF16)],
        compiler_params=_cparams(("parallel", "parallel")),
        name="even_proj",
    )(x, shift, scale, w_pad, cos, sin)


def _odd_proj_kernel(x_ref, sh_ref, sc_ref, w_ref, bf_ref,
                     kt_ref, vt_ref, lft_ref, qb_ref, ktb_ref, vb_ref, fq_ref, ft_ref, carry_sc):
    t = pl.program_id(1)
    tm = x_ref.shape[1]
    h = (x_ref[0] * (1.0 + sc_ref[0]) + sh_ref[0]).astype(BF16)

    def proj(lo, width):
        return _dot(h, w_ref[:, lo:lo + width])

    q = proj(0, D_MODEL)
    qb_ref[0] = (q * (HEAD_DIM ** -0.5 * LOG2E)).astype(BF16)
    k = proj(D_MODEL, D_MODEL).T
    kt_ref[0] = k
    ktb_ref[0] = k.astype(BF16)
    v = proj(2 * D_MODEL, D_MODEL)
    vt_ref[0] = v.T
    vb_ref[0] = v.astype(BF16)
    zf = proj(3 * D_MODEL, LANES) + bf_ref[...]
    logf = jnp.minimum(zf, 0.0) - _softplus_tail(zf)
    lft_ref[0] = logf.T[0:N_HEADS, :]

    @pl.when(t == 0)
    def _():
        carry_sc[...] = jnp.zeros_like(carry_sc)

    cum = _dot3_left(_tri(tm, "ge"), logf) + carry_sc[...]
    carry_sc[...] = cum[tm - 1:tm, :]
    cum2 = cum * LOG2E
    fq_ref[0] = cum2[:, 0:N_HEADS]
    ft_ref[0] = cum2.T[0:N_HEADS, :]


def _odd_proj(x, shift, scale, w_pad, bf_pad):
    g, tg, _ = x.shape
    tm = min(512, tg)
    nt = tg // tm
    row = lambda width: pl.BlockSpec((1, tm, width), lambda g_, t: (g_, t, 0))
    col = lambda height: pl.BlockSpec((1, height, tm), lambda g_, t: (g_, 0, t))
    rows = lambda width, dt: jax.ShapeDtypeStruct((g, tg, width), dt)
    cols = lambda height, dt: jax.ShapeDtypeStruct((g, height, tg), dt)
    return pl.pallas_call(
        _odd_proj_kernel,
        grid=(g, nt),
        in_specs=[row(D_MODEL), _mod_spec(shift, tm, 2), _mod_spec(scale, tm, 2),
                  pl.BlockSpec((D_MODEL, P_ODD_PAD), lambda g_, t: (0, 0)),
                  pl.BlockSpec((1, LANES), lambda g_, t: (0, 0))],
        out_specs=[col(D_MODEL), col(D_MODEL), col(N_HEADS), row(D_MODEL), col(D_MODEL), row(D_MODEL),
                   row(N_HEADS), col(N_HEADS)],
        out_shape=[cols(D_MODEL, F32), cols(D_MODEL, F32), cols(N_HEADS, F32), rows(D_MODEL, BF16),
                   cols(D_MODEL, BF16), rows(D_MODEL, BF16), rows(N_HEADS, F32), cols(N_HEADS, F32)],
        scratch_shapes=[pltpu.VMEM((1, LANES), F32)],
        compiler_params=_cparams(("parallel", "arbitrary")),
        name="odd_proj",
    )(x, shift, scale, w_pad, bf_pad)


def _out_proj_kernel(n_parts, *refs):
    o_refs = refs[:n_parts]
    w_refs = refs[n_parts:2 * n_parts]
    x_ref, gt_ref, g_ref, b_ref, y_ref = refs[2 * n_parts:]
    out = _dot(o_refs[0][0], w_refs[0][...])
    for i in range(1, n_parts):
        out = out + _dot(o_refs[i][0], w_refs[i][...])
    y = ALPHA * x_ref[0] + gt_ref[0] * out
    y_ref[0] = _layer_norm(y, g_ref[...], b_ref[...])


def _out_proj(o_parts, w_parts, x, gate, ln_g, ln_b):
    g, tg, _ = x.shape
    tm = min(512, tg)
    n = len(o_parts)
    row = lambda width: pl.BlockSpec((1, tm, width), lambda g_, t: (g_, t, 0))
    vec = pl.BlockSpec((1, D_MODEL), lambda g_, t: (0, 0))
    in_specs = ([row(o.shape[-1]) for o in o_parts]
                + [pl.BlockSpec(w.shape, lambda g_, t: (0, 0)) for w in w_parts]
                + [row(D_MODEL), _mod_spec(gate, tm, 2), vec, vec])
    return pl.pallas_call(
        functools.partial(_out_proj_kernel, n),
        grid=(g, tg // tm),
        in_specs=in_specs,
        out_specs=row(D_MODEL),
        out_shape=jax.ShapeDtypeStruct(x.shape, F32),
        compiler_params=_cparams(("parallel", "parallel")),
        name="out_proj",
    )(*o_parts, *w_parts, x, gate, ln_g.reshape(1, D_MODEL), ln_b.reshape(1, D_MODEL))


def _half_masks(shape):
    lane = lax.broadcasted_iota(I32, shape, 1)
    return lane < HEAD_DIM, lane >= HEAD_DIM


def _sb_kernel(q_ref, kt_ref, v_ref, o_ref, acc_sc):
    tq = q_ref.shape[1]
    tk = tq
    i = pl.program_id(1)
    qpos = i * tq + lax.broadcasted_iota(I32, (tq, 1), 0)
    halves = _half_masks((tq, LANES))
    acc_sc[...] = jnp.zeros_like(acc_sc)

    def cond(c):
        return jnp.logical_and(c[0] >= 0, c[1] > 0)

    def body(c):
        kb, _, carries = c
        start = pl.multiple_of(kb * tk, tk)
        kpos = start + lax.broadcasted_iota(I32, (1, tk), 1)
        vis = kpos < qpos
        u = _tri(tk, "gt")
        new_carries = []
        for p in range(N_SB_PAIRS):
            cols = slice(p * LANES, (p + 1) * LANES)
            q = q_ref[0, :, cols]
            kt = kt_ref[0, cols, pl.ds(start, tk)]
            v = v_ref[0, pl.ds(start, tk), cols]
            for hh in range(2):
                h = 2 * p + hh
                z = _dot(jnp.where(halves[hh], q, jnp.zeros_like(q)), kt)
                tail = _softplus_tail(z)
                lm = jnp.where(vis, -(jnp.maximum(z, 0.0) + tail), 0.0)
                local = _dot3_right(lm, u)
                w = jnp.where(vis, jnp.exp(jnp.minimum(z, 0.0) - tail + local + carries[h]), 0.0)
                acc_sc[h] += _dot(w.astype(BF16), v)
                new_carries.append(carries[h] + local[:, 0:1] + lm[:, 0:1])
        top = new_carries[0]
        for cr in new_carries[1:]:
            top = jnp.maximum(top, cr)
        go = (jnp.max(top) >= SB_SKIP).astype(I32)
        return kb - 1, go, tuple(new_carries)

    zc = jnp.zeros((tq, 1), F32)
    lax.while_loop(cond, body, (i, jnp.int32(1), (zc,) * N_SB_HEADS))
    for p in range(N_SB_PAIRS):
        o_ref[0, :, p * LANES:(p + 1) * LANES] = jnp.where(
            halves[0], acc_sc[2 * p], acc_sc[2 * p + 1]).astype(o_ref.dtype)


def _sb_attention(qb, kt, vb):
    b, t, _ = qb.shape
    tq = 128
    return pl.pallas_call(
        _sb_kernel,
        grid=(b, t // tq),
        in_specs=[pl.BlockSpec((1, tq, W_HALF), lambda b_, i: (b_, i, 0)),
                  pl.BlockSpec((1, W_HALF, t), lambda b_, i: (b_, 0, 0)),
                  pl.BlockSpec((1, t, W_HALF), lambda b_, i: (b_, 0, 0))],
        out_specs=pl.BlockSpec((1, tq, W_HALF), lambda b_, i: (b_, i, 0)),
        out_shape=jax.ShapeDtypeStruct((b, t, W_HALF), BF16),
        scratch_shapes=[pltpu.VMEM((N_SB_HEADS, tq, LANES), F32)],
        compiler_params=_cparams(("parallel", "parallel")),
        name="sb_attention",
    )(qb, kt, vb)


def _flash_kernel(mode, tk, *refs):
    if mode == "dsa":
        q_ref, kt_ref, v_ref, sel_ref, o_ref = refs
    else:
        q_ref, kt_ref, v_ref, fq_ref, ft_ref, o_ref = refs
    tq = q_ref.shape[1]
    p = pl.program_id(1)
    i = pl.program_id(2)
    q = q_ref[0]
    qpos = i * tq + lax.broadcasted_iota(I32, (tq, 1), 0)
    n_full = (i * tq) // tk
    n_kb = (i * tq + tq + tk - 1) // tk
    halves = _half_masks((tq, LANES))
    qh = [jnp.where(halves[hh], q, jnp.zeros_like(q)) for hh in range(2)]
    if mode == "fox":
        hl = lax.broadcasted_iota(I32, (tq, N_HEADS), 1)
        fq = [jnp.sum(jnp.where(hl == 2 * p + hh, fq_ref[0], 0.0), axis=1, keepdims=True)
              for hh in range(2)]

    def block(diagonal, kb, c):
        start = pl.multiple_of(kb * tk, tk)
        kt = kt_ref[0, :, pl.ds(start, tk)]
        v = v_ref[0, pl.ds(start, tk), :]
        keep = None
        if diagonal:
            keep = (start + lax.broadcasted_iota(I32, (1, tk), 1)) <= qpos
        if mode == "dsa":
            sel = sel_ref[0, :, pl.ds(start, tk)] > 0
            keep = sel if keep is None else jnp.logical_and(keep, sel)
        out = []
        for hh in range(2):
            m, l, acc = c[hh]
            s = _dot(qh[hh], kt)
            if mode == "fox":
                s = s + (fq[hh] - ft_ref[0, pl.ds(2 * p + hh, 1), pl.ds(start, tk)])
            if keep is not None:
                s = jnp.where(keep, s, NEG)
            m_new = jnp.maximum(m, jnp.max(s, axis=1, keepdims=True))
            a = jnp.exp2(m - m_new)
            pr = jnp.exp2(s - m_new)
            l = a * l + jnp.sum(pr, axis=1, keepdims=True)
            acc = a * acc + _dot(pr.astype(BF16), v)
            out.append((m_new, l, acc))
        return tuple(out)

    one = (jnp.full((tq, 1), NEG, F32), jnp.zeros((tq, 1), F32), jnp.zeros((tq, LANES), F32))
    c = lax.fori_loop(0, n_full, functools.partial(block, False), (one, one))
    c = lax.fori_loop(n_full, n_kb, functools.partial(block, True), c)
    o_ref[0] = jnp.where(halves[0], c[0][2] / c[0][1], c[1][2] / c[1][1]).astype(o_ref.dtype)


def _flash_attention(mode, qb, kt, vb, extra, pair0, n_pairs):
    b, t, _ = qb.shape
    tq = 256
    tk = min(512, t)
    in_specs = [pl.BlockSpec((1, tq, LANES), lambda b_, p, i: (b_, i, pair0 + p)),
                pl.BlockSpec((1, LANES, t), lambda b_, p, i: (b_, pair0 + p, 0)),
                pl.BlockSpec((1, t, LANES), lambda b_, p, i: (b_, 0, pair0 + p))]
    if mode == "dsa":
        in_specs.append(pl.BlockSpec((1, tq, t), lambda b_, p, i: (b_, i, 0)))
    else:
        in_specs += [pl.BlockSpec((1, tq, N_HEADS), lambda b_, p, i: (b_, i, 0)),
                     pl.BlockSpec((1, N_HEADS, t), lambda b_, p, i: (b_, 0, 0))]
    return pl.pallas_call(
        functools.partial(_flash_kernel, mode, tk),
        grid=(b, n_pairs, t // tq),
        in_specs=in_specs,
        out_specs=pl.BlockSpec((1, tq, LANES), lambda b_, p, i: (b_, i, p)),
        out_shape=jax.ShapeDtypeStruct((b, t, n_pairs * LANES), BF16),
        compiler_params=_cparams(("parallel", "parallel", "parallel")),
        name=mode + "_attention",
    )(qb, kt, vb, *extra)


def _idx_scores(qix, wix, kixt):
    score = None
    for hi in range(N_IDX_HEADS):
        s = _dot(qix[:, hi * IDX_DIM:(hi + 1) * IDX_DIM], kixt)
        term = wix[:, hi:hi + 1] * jnp.maximum(s, 0.0)
        score = term if score is None else score + term
    return score


def _sort_key(score):
    bits = lax.bitcast_convert_type(score, I32)
    sign = bits >> 31
    return ((bits & jnp.int32(0x7FFFFFFF)) ^ sign) - sign


def _select_topk(keys_ref, out_ref, n_chunks, top):
    rows, width = keys_ref.shape
    total_chunks = width // SEL_CHUNK
    sub = SEL_CHUNK // LANES

    def count(pred, cand):
        def body(c, acc):
            base = c * SEL_CHUNK
            for s_ in range(sub):
                kc = keys_ref[:, pl.ds(pl.multiple_of(base + s_ * LANES, LANES), LANES)]
                acc = acc + jnp.where(pred(kc, cand), 1, 0)
            return acc
        acc = lax.fori_loop(0, n_chunks, body, jnp.zeros((rows, LANES), I32))
        return jnp.sum(acc, axis=1, keepdims=True)

    ge = lambda kc, cand: kc >= cand
    zero = jnp.zeros((rows, 1), I32)
    thr = jnp.where(count(ge, zero) >= top, zero, jnp.full((rows, 1), INT_MIN, I32))

    def bit_body(bi, thr):
        cand = thr | (jnp.int32(1) << (30 - bi))
        return jnp.where(count(ge, cand) >= top, cand, thr)

    thr = lax.fori_loop(0, 31, bit_body, thr)
    need = (top - count(lambda kc, cand: kc > cand, thr)).astype(F32)
    u = _tri(LANES, "le")

    def sel_body(c, seen):
        base = c * SEL_CHUNK
        for s_ in range(sub):
            off = pl.multiple_of(base + s_ * LANES, LANES)
            kc = keys_ref[:, pl.ds(off, LANES)]
            is_eq = kc == thr
            eq = jnp.where(is_eq, 1.0, 0.0)
            upto = _dot(eq.astype(BF16), u) + seen
            sel = jnp.logical_or(kc > thr, jnp.logical_and(is_eq, upto - eq < need))
            out_ref[:, pl.ds(off, LANES)] = jnp.where(sel, 1.0, 0.0).astype(out_ref.dtype)
            seen = seen + jnp.sum(eq, axis=1, keepdims=True)
        return seen

    lax.fori_loop(0, n_chunks, sel_body, jnp.zeros((rows, 1), F32))

    def zero_body(c, _):
        off = pl.multiple_of(c * SEL_CHUNK, SEL_CHUNK)
        out_ref[:, pl.ds(off, SEL_CHUNK)] = jnp.zeros((rows, SEL_CHUNK), out_ref.dtype)
        return 0

    lax.fori_loop(n_chunks, total_chunks, zero_body, 0)


def _prompt_select_kernel(top, qix_ref, wix_ref, kixt_ref, sel_ref, keys_sc):
    tq = qix_ref.shape[1]
    i = pl.program_id(1)
    qix = qix_ref[0]
    wix = wix_ref[0]
    qpos = i * tq + lax.broadcasted_iota(I32, (tq, 1), 0)
    n_chunks = (i * tq + tq + SEL_CHUNK - 1) // SEL_CHUNK

    def score_body(c, _):
        start = pl.multiple_of(c * SEL_CHUNK, SEL_CHUNK)
        score = _idx_scores(qix, wix, kixt_ref[0, :, pl.ds(start, SEL_CHUNK)])
        kpos = start + lax.broadcasted_iota(I32, (1, SEL_CHUNK), 1)
        score = jnp.where(kpos <= qpos, score, -jnp.inf)
        keys_sc[:, pl.ds(start, SEL_CHUNK)] = _sort_key(score)
        return 0

    lax.fori_loop(0, n_chunks, score_body, 0)
    _select_topk(keys_sc, sel_ref.at[0], n_chunks, top)


def _prompt_select(qix, wix, kixtb):
    b, t, _ = qix.shape
    tq = 128
    top = min(DSA_TOPK, t // 4)
    return pl.pallas_call(
        functools.partial(_prompt_select_kernel, top),
        grid=(b, t // tq),
        in_specs=[pl.BlockSpec((1, tq, N_IDX_HEADS * IDX_DIM), lambda b_, i: (b_, i, 0)),
                  pl.BlockSpec((1, tq, N_IDX_HEADS), lambda b_, i: (b_, i, 0)),
                  pl.BlockSpec((1, IDX_DIM, t), lambda b_, i: (b_, 0, 0))],
        out_specs=pl.BlockSpec((1, tq, t), lambda b_, i: (b_, i, 0)),
        out_shape=jax.ShapeDtypeStruct((b, t, t), BF16),
        scratch_shapes=[pltpu.VMEM((tq, t), I32)],
        compiler_params=_cparams(("parallel", "parallel")),
        name="prompt_select",
    )(qix, wix, kixtb)


def _dec_scores_kernel(n_steps, past, pt_ref, qix_ref, wix_ref, *refs):
    page_refs = refs[:DEC_SCORE_PAGES]
    kn_ref, s_ref = refs[DEC_SCORE_PAGES:]
    j = pl.program_id(1)
    rows = qix_ref.shape[1]
    width = s_ref.shape[2]
    qpos = past + lax.broadcasted_iota(I32, (rows, 1), 0)
    kpos = j * width + lax.broadcasted_iota(I32, (1, width), 1)

    def emit(kixt):
        score = _idx_scores(qix_ref[0], wix_ref[0], kixt.astype(BF16))
        s_ref[0] = jnp.where(kpos <= qpos, score, -jnp.inf)

    @pl.when(j < n_steps)
    def _():
        emit(jnp.concatenate([r[...] for r in page_refs], axis=1))

    @pl.when(j == n_steps)
    def _():
        emit(kn_ref[0])


def _dec_scores(page_table, qix, wix, kix_cache_t, kix_new_t, past):
    db, n_pages = page_table.shape
    rows = qix.shape[1]
    width = DEC_SCORE_PAGES * PAGE
    n_steps = n_pages // DEC_SCORE_PAGES

    def page_spec(k):
        return pl.BlockSpec((None, IDX_DIM, PAGE),
                            lambda b, j, pt: (pt[b, jnp.minimum(j, n_steps - 1) * DEC_SCORE_PAGES + k], 0, 0))

    grid_spec = pltpu.PrefetchScalarGridSpec(
        num_scalar_prefetch=1,
        grid=(db, n_steps + 1),
        in_specs=[pl.BlockSpec((1, rows, N_IDX_HEADS * IDX_DIM), lambda b, j, pt: (b, 0, 0)),
                  pl.BlockSpec((1, rows, N_IDX_HEADS), lambda b, j, pt: (b, 0, 0))]
                 + [page_spec(k) for k in range(DEC_SCORE_PAGES)]
                 + [pl.BlockSpec((1, IDX_DIM, width), lambda b, j, pt: (b, 0, 0))],
        out_specs=pl.BlockSpec((1, rows, width), lambda b, j, pt: (b, 0, j)))
    return pl.pallas_call(
        functools.partial(_dec_scores_kernel, n_steps, past),
        grid_spec=grid_spec,
        out_shape=jax.ShapeDtypeStruct((db, rows, (n_steps + 1) * width), F32),
        compiler_params=_cparams(("parallel", "arbitrary")),
        name="dec_scores",
    )(page_table, qix, wix, *([kix_cache_t] * DEC_SCORE_PAGES), kix_new_t)


def _dec_select_kernel(top, s_ref, sel_ref, keys_sc):
    keys_sc[...] = _sort_key(s_ref[...])
    _select_topk(keys_sc, sel_ref, s_ref.shape[1] // SEL_CHUNK, top)


def _dec_select(scores, top):
    r, width = scores.shape
    tr = min(128, r)
    return pl.pallas_call(
        functools.partial(_dec_select_kernel, top),
        grid=(r // tr,),
        in_specs=[pl.BlockSpec((tr, width), lambda i: (i, 0))],
        out_specs=pl.BlockSpec((tr, width), lambda i: (i, 0)),
        out_shape=jax.ShapeDtypeStruct((r, width), BF16),
        scratch_shapes=[pltpu.VMEM((tr, width), I32)],
        compiler_params=_cparams(("parallel",)),
        name="dec_select",
    )(scores)


def _dec_attn_kernel(mode, n_steps, past, n_q, pt_ref, *refs):
    np_ = DEC_PAGES
    q_ref = refs[0]
    k_refs = refs[1:1 + np_]
    v_refs = refs[1 + np_:1 + 2 * np_]
    kn_ref, vn_ref = refs[1 + 2 * np_:3 + 2 * np_]
    rest = refs[3 + 2 * np_:]
    if mode == "even":
        sel_ref, seln_ref, o_ref, m_sc, l_sc, acc_sc, carry_sc = rest
    else:
        lf_refs = rest[:np_]
        lfn_ref, o_ref, m_sc, l_sc, acc_sc, carry_sc, sq_sc = rest[np_:]
    j = pl.program_id(1)
    n_sb = N_SB_HEADS * n_q if mode == "even" else 0
    row = lax.broadcasted_iota(I32, (DEC_ROWS, 1), 0)
    qpos = past + row % n_q

    @pl.when(j == 0)
    def _():
        m_sc[...] = jnp.full_like(m_sc, NEG)
        l_sc[...] = jnp.zeros_like(l_sc)
        acc_sc[...] = jnp.zeros_like(acc_sc)
        carry_sc[...] = jnp.zeros_like(carry_sc)

    def softmax_rows(s, valid, m_old, l_old):
        s = jnp.where(valid, s, NEG)
        m_new = jnp.maximum(m_old, jnp.max(s, axis=1, keepdims=True))
        a = jnp.exp2(m_old - m_new)
        pr = jnp.where(valid, jnp.exp2(s - m_new), 0.0)
        return pr, a, m_new, a * l_old + jnp.sum(pr, axis=1, keepdims=True)

    def step(kt, vt, key0, extra, first):
        w_ = kt.shape[1]
        s = _dot(q_ref[0], kt.astype(BF16))
        kpos = key0 + lax.broadcasted_iota(I32, (1, w_), 1)
        if mode == "even":
            z = s[0:n_sb]
            vis = kpos < qpos[0:n_sb]
            tail = _softplus_tail(z)
            lm = jnp.where(vis, -(jnp.maximum(z, 0.0) + tail), 0.0)
            rest_, carry = _suffix_sums(lm, carry_sc[...])
            carry_sc[...] = carry
            w = jnp.where(vis, jnp.exp(jnp.minimum(z, 0.0) - tail + rest_), 0.0)
            r_ = lax.broadcasted_iota(I32, (DEC_ROWS - n_sb, DEC_QPAD), 0)
            c_ = lax.broadcasted_iota(I32, (DEC_ROWS - n_sb, DEC_QPAD), 1)
            spread = jnp.where(c_ == r_ % n_q, 1.0, 0.0).astype(BF16)
            selx = _dot(spread, extra) > 0.5
            valid = jnp.logical_and(kpos <= qpos[n_sb:], selx)
            pr, a, m_new, l_new = softmax_rows(s[n_sb:], valid, m_sc[n_sb:], l_sc[n_sb:])
            m_sc[n_sb:] = m_new
            l_sc[n_sb:] = l_new
            p_all = jnp.concatenate([w, pr], axis=0)
            a_all = jnp.concatenate([jnp.ones((n_sb, 1), F32), a], axis=0)
        else:
            suf, carry = _suffix_sums(extra, carry_sc[...])
            carry_sc[...] = carry
            r_ = lax.broadcasted_iota(I32, (DEC_ROWS, N_HEADS), 0)
            c_ = lax.broadcasted_iota(I32, (DEC_ROWS, N_HEADS), 1)
            spread = jnp.where(c_ == r_ // n_q, 1.0, 0.0).astype(BF16)
            suf_rows = _dot3_left(spread, suf)
            if first:
                lane = lax.broadcasted_iota(I32, (1, w_), 1)
                sq_sc[...] = jnp.sum(jnp.where(lane == row % n_q, suf_rows, 0.0), axis=1, keepdims=True)
            logit = s + (suf_rows - sq_sc[...]) * LOG2E
            p_all, a_all, m_new, l_new = softmax_rows(logit, kpos <= qpos, m_sc[...], l_sc[...])
            m_sc[...] = m_new
            l_sc[...] = l_new
        acc_sc[...] = acc_sc[...] * _col_to_row(a_all) + _dot_nt(vt.astype(BF16), p_all.astype(BF16))

    @pl.when(j == 0)
    def _():
        extra = seln_ref[0] if mode == "even" else lfn_ref[0]
        step(kn_ref[0], vn_ref[0], past, extra, True)

    @pl.when(j > 0)
    def _():
        kt = jnp.concatenate([r[...] for r in k_refs], axis=1)
        vt = jnp.concatenate([r[...] for r in v_refs], axis=1)
        if mode == "even":
            extra = sel_ref[0]
        else:
            extra = jnp.concatenate([r[...] for r in lf_refs], axis=1)
        step(kt, vt, past - j * (np_ * PAGE), extra, False)

    @pl.when(j == n_steps)
    def _():
        l = jnp.where(row < n_sb, 1.0, l_sc[...])
        hr = lax.broadcasted_iota(I32, (D_MODEL, DEC_ROWS), 0) // HEAD_DIM
        hc = lax.broadcasted_iota(I32, (D_MODEL, DEC_ROWS), 1) // n_q
        own = jnp.where(hr == hc, acc_sc[...], 0.0).reshape(N_HEADS, HEAD_DIM, DEC_ROWS)
        o_ref[0] = jnp.sum(own, axis=0) / _col_to_row(l)


def _dec_attention(mode, page_table, qbd, k_cache_t, v_cache_t, k_new_t, v_new_t, extra, past, n_q):
    db, n_pages = page_table.shape
    np_ = DEC_PAGES
    n_steps = n_pages // np_
    width = np_ * PAGE
    per_b = lambda b, j, pt: (b, 0, 0)

    def page_spec(k, feat):
        return pl.BlockSpec((None, feat, PAGE),
                            lambda b, j, pt: (pt[b, n_pages - jnp.maximum(j, 1) * np_ + k], 0, 0))

    in_specs = ([pl.BlockSpec((1, DEC_ROWS, D_MODEL), per_b)]
                + [page_spec(k, D_MODEL) for k in range(np_)]
                + [page_spec(k, D_MODEL) for k in range(np_)]
                + [pl.BlockSpec((1, D_MODEL, PAGE), per_b), pl.BlockSpec((1, D_MODEL, PAGE), per_b)])
    scratch = [pltpu.VMEM((DEC_ROWS, 1), F32), pltpu.VMEM((DEC_ROWS, 1), F32),
               pltpu.VMEM((D_MODEL, DEC_ROWS), F32)]
    if mode == "even":
        (sel,) = extra
        rows = sel.shape[1]
        in_specs += [pl.BlockSpec((1, rows, width), lambda b, j, pt: (b, 0, n_steps - jnp.maximum(j, 1))),
                     pl.BlockSpec((1, rows, PAGE), per_b)]
        operands = [sel, sel[:, :, past:past + PAGE]]
        scratch.append(pltpu.VMEM((N_SB_HEADS * n_q, 1), F32))
    else:
        logf_cache_t, logf_new_t = extra
        in_specs += [page_spec(k, N_HEADS) for k in range(np_)]
        in_specs.append(pl.BlockSpec((1, N_HEADS, PAGE), per_b))
        operands = [logf_cache_t] * np_ + [logf_new_t]
        scratch += [pltpu.VMEM((N_HEADS, 1), F32), pltpu.VMEM((DEC_ROWS, 1), F32)]
    grid_spec = pltpu.PrefetchScalarGridSpec(
        num_scalar_prefetch=1,
        grid=(db, n_steps + 1),
        in_specs=in_specs,
        out_specs=pl.BlockSpec((1, HEAD_DIM, DEC_ROWS), per_b),
        scratch_shapes=scratch)
    return pl.pallas_call(
        functools.partial(_dec_attn_kernel, mode, n_steps, past, n_q),
        grid_spec=grid_spec,
        out_shape=jax.ShapeDtypeStruct((db, HEAD_DIM, DEC_ROWS), F32),
        compiler_params=_cparams(("parallel", "arbitrary")),
        name="dec_attention_" + mode,
    )(page_table, qbd, *([k_cache_t] * np_), *([v_cache_t] * np_), k_new_t, v_new_t, *operands)


def _rope_tables(pos):
    half = HEAD_DIM // 2
    inv = ROPE_THETA ** (-jnp.arange(half, dtype=F32) / half)
    ang = pos.astype(F32)[:, None] * inv
    cos, sin = jnp.cos(ang), jnp.sin(ang)
    reps = W_HALF // HEAD_DIM
    return (jnp.tile(jnp.concatenate([cos, cos], axis=-1), (1, reps)),
            jnp.tile(jnp.concatenate([-sin, sin], axis=-1), (1, reps)))


def _heads_view(xt):
    g, _, t = xt.shape
    return xt.reshape(g, N_HEADS, HEAD_DIM, t).transpose(0, 3, 1, 2)


def _new_block(xt, db, n_q, width):
    f = xt.shape[0]
    return jnp.pad(xt.reshape(f, db, n_q).transpose(1, 0, 2), ((0, 0), (0, 0), (0, width - n_q)))


def _block_diag_queries(qb, db, n_q):
    q = qb.reshape(db, n_q, N_HEADS, HEAD_DIM).transpose(0, 2, 1, 3)
    eye = jnp.eye(N_HEADS, dtype=qb.dtype)
    q = q[:, :, :, None, :] * eye[None, :, None, :, None]
    q = q.reshape(db, N_HEADS * n_q, D_MODEL)
    return jnp.pad(q, ((0, 0), (0, DEC_ROWS - N_HEADS * n_q), (0, 0)))


def _rows_token(o, db, n_q):
    o = o[:, :, :N_HEADS * n_q].reshape(db, HEAD_DIM, N_HEADS, n_q)
    return o.transpose(0, 3, 2, 1).reshape(1, db * n_q, D_MODEL)


def kernel(x_prompt, x_sample, cache_k_even, cache_v_even, cache_kidx_even, cache_k_odd, cache_v_odd,
           cache_logf_odd, page_table, c_prompt, c_sample, w_ada, b_ada, ln_g, ln_b, ffn_w1, ffn_w3,
           ffn_w2, w_in_even, w_out_even, w_in_odd, b_forget, w_out_odd):
    b, t, _ = x_prompt.shape
    db, n_q, _ = x_sample.shape
    n_pages = page_table.shape[1]
    past = n_pages * PAGE
    pool = cache_k_even.shape[1]

    n_c = b + db
    c_all = jnp.pad(jnp.concatenate([c_prompt, c_sample], axis=0), ((0, (-n_c) % 8), (0, 0)))
    mod = _ada_mod(c_all, w_ada, b_ada).reshape(DEPTH, -1, N_SUB, 3, D_MODEL)

    def mods(layer, sub):
        m = mod[layer, :, sub]
        mp = [m[:b, i].reshape(b, 1, D_MODEL) for i in range(3)]
        ms = [jnp.repeat(m[b:n_c, i], n_q, axis=0).reshape(1, db * n_q, D_MODEL) for i in range(3)]
        return mp, ms

    cos_p, sin_p = _rope_tables(jnp.tile(jnp.arange(t, dtype=I32), b))
    cos_s, sin_s = _rope_tables(jnp.tile(past + jnp.arange(n_q, dtype=I32), db))

    w_even = jnp.pad(w_in_even, ((0, 0), (0, 0), (0, P_EVEN_PAD - P_EVEN))).astype(BF16)
    w_odd = jnp.pad(w_in_odd, ((0, 0), (0, 0), (0, P_ODD_PAD - w_in_odd.shape[-1]))).astype(BF16)
    bf_pad = jnp.pad(b_forget, ((0, 0), (0, LANES - N_HEADS)))
    w1, w3, w2 = ffn_w1.astype(BF16), ffn_w3.astype(BF16), ffn_w2.astype(BF16)
    wo_even, wo_odd = w_out_even.astype(BF16), w_out_odd.astype(BF16)

    def cache_t(cache, j, feat):
        c = cache[j].reshape(pool, PAGE, feat)
        return c.transpose(0, 2, 1)

    top_s = min(DSA_TOPK, (past + n_q) // 4)
    score_w = DEC_SCORE_PAGES * PAGE

    xp = x_prompt
    xs = x_sample.reshape(1, db * n_q, D_MODEL)
    even_p, odd_p, even_s, odd_s = [], [], [], []
    for layer in range(DEPTH):
        j = layer // 2
        (mp0, ms0), (mp1, ms1), (mp2, ms2) = mods(layer, 0), mods(layer, 1), mods(layer, 2)
        ffn_a = (w1[layer, 0], w3[layer, 0], w2[layer, 0], ln_g[layer, 0], ln_b[layer, 0])
        xp = _ffn_sublayer(xp, *mp0, *ffn_a)
        xs = _ffn_sublayer(xs, *ms0, *ffn_a)
        if layer % 2 == 0:
            kt32, vt32, kixt32, qb, ktb, vb, qix, wix, kixtb = _even_proj(
                xp, mp1[0], mp1[1], w_even[j], cos_p, sin_p)
            even_p.append((_heads_view(kt32), _heads_view(vt32), kixt32.transpose(0, 2, 1)))
            o_sb = _sb_attention(qb, ktb, vb)
            sel = _prompt_select(qix, wix, kixtb)
            o_ds = _flash_attention("dsa", qb, ktb, vb, (sel,), N_SB_PAIRS, N_PAIRS - N_SB_PAIRS)
            xp = _out_proj((o_sb, o_ds), (wo_even[j, :W_HALF], wo_even[j, W_HALF:]), xp, mp1[2],
                           ln_g[layer, 1], ln_b[layer, 1])
            kt32, vt32, kixt32, qb, _, _, qix, wix, _ = _even_proj(xs, ms1[0], ms1[1], w_even[j], cos_s, sin_s)
            even_s.append((kt32[0].T.reshape(db, n_q, N_HEADS, HEAD_DIM),
                           vt32[0].T.reshape(db, n_q, N_HEADS, HEAD_DIM),
                           kixt32[0].T.reshape(db, n_q, IDX_DIM)))
            pad_q = lambda a: jnp.pad(a.reshape(db, n_q, -1), ((0, 0), (0, DEC_QPAD - n_q), (0, 0)))
            scores = _dec_scores(page_table, pad_q(qix), pad_q(wix), cache_t(cache_kidx_even, j, IDX_DIM),
                                 _new_block(kixt32[0], db, n_q, score_w), past)
            sel = _dec_select(scores.reshape(db * DEC_QPAD, -1), top_s).reshape(db, DEC_QPAD, -1)
            o = _dec_attention(
                "even", page_table, _block_diag_queries(qb[0], db, n_q),
                cache_t(cache_k_even, j, D_MODEL), cache_t(cache_v_even, j, D_MODEL),
                _new_block(kt32[0], db, n_q, PAGE), _new_block(vt32[0], db, n_q, PAGE),
                (sel,), past, n_q)
            xs = _out_proj((_rows_token(o, db, n_q).astype(BF16),), (wo_even[j],), xs, ms1[2],
                           ln_g[layer, 1], ln_b[layer, 1])
        else:
            kt32, vt32, lft, qb, ktb, vb, fq, ft = _odd_proj(xp, mp1[0], mp1[1], w_odd[j], bf_pad[j:j + 1])
            odd_p.append((_heads_view(kt32), _heads_view(vt32), lft.transpose(0, 2, 1)))
            o = _flash_attention("fox", qb, ktb, vb, (fq, ft), 0, N_PAIRS)
            xp = _out_proj((o,), (wo_odd[j],), xp, mp1[2], ln_g[layer, 1], ln_b[layer, 1])
            kt32, vt32, lft, qb, _, _, _, _ = _odd_proj(xs, ms1[0], ms1[1], w_odd[j], bf_pad[j:j + 1])
            odd_s.append((kt32[0].T.reshape(db, n_q, N_HEADS, HEAD_DIM),
                          vt32[0].T.reshape(db, n_q, N_HEADS, HEAD_DIM),
                          lft[0].T.reshape(db, n_q, N_HEADS)))
            o = _dec_attention(
                "odd", page_table, _block_diag_queries(qb[0], db, n_q),
                cache_t(cache_k_odd, j, D_MODEL), cache_t(cache_v_odd, j, D_MODEL),
                _new_block(kt32[0], db, n_q, PAGE), _new_block(vt32[0], db, n_q, PAGE),
                (cache_t(cache_logf_odd, j, N_HEADS), _new_block(lft[0], db, n_q, PAGE)), past, n_q)
            xs = _out_proj((_rows_token(o, db, n_q).astype(BF16),), (wo_odd[j],), xs, ms1[2],
                           ln_g[layer, 1], ln_b[layer, 1])
        ffn_b = (w1[layer, 1], w3[layer, 1], w2[layer, 1], ln_g[layer, 2], ln_b[layer, 2])
        xp = _ffn_sublayer(xp, *mp2, *ffn_b)
        xs = _ffn_sublayer(xs, *ms2, *ffn_b)

    def stack(rows, i):
        return jnp.stack([r[i] for r in rows])

    return (xp, xs.reshape(db, n_q, D_MODEL),
            stack(even_p, 0), stack(even_p, 1), stack(even_p, 2),
            stack(odd_p, 0), stack(odd_p, 1), stack(odd_p, 2),
            stack(even_s, 0), stack(even_s, 1), stack(even_s, 2),
            stack(odd_s, 0), stack(odd_s, 1), stack(odd_s, 2))
```

```python
import functools
import math

import jax
import jax.numpy as jnp
from jax import lax
from jax.experimental import pallas as pl
from jax.experimental.pallas import tpu as pltpu

F32 = jnp.float32
BF16 = jnp.bfloat16
I32 = jnp.int32

D_MODEL = 1024
HEAD_DIM = 64
N_HEADS = 16
N_SB_HEADS = 8
N_PAIRS = N_HEADS // 2
N_SB_PAIRS = N_SB_HEADS // 2
W_HALF = N_SB_HEADS * HEAD_DIM
N_IDX_HEADS = 4
IDX_DIM = 64
DSA_TOPK = 256
D_FF = 2816
ROPE_THETA = 10000.0
PAGE = 128
N_SUB = 3
DEPTH = 2
ALPHA = (2 * DEPTH) ** 0.25
LN_EPS = 1e-5
LANES = 128
P_EVEN = 6 * W_HALF + N_IDX_HEADS * IDX_DIM + IDX_DIM + N_IDX_HEADS
P_EVEN_PAD = 3456
IDX_COL = 6 * W_HALF
IDX_W = P_EVEN_PAD - IDX_COL
P_ODD_PAD = 3 * D_MODEL + LANES
NEG = -1e30
SB_SKIP = -105.0
INT_MIN = -(2 ** 31)
LOG2E = math.log2(math.e)
SEL_CHUNK = 512
DEC_ROWS = 128
SEL_TQ = 128
SB_TQ = 256
SB_TK = 256
FLASH_TQ = 512
FLASH_TK = 1024
DEC_QPAD = 16
DEC_PAGES = 8
DEC_SCORE_PAGES = 8
VMEM_LIMIT = 56 * 1024 * 1024


def _cparams(sem):
    return pltpu.CompilerParams(dimension_semantics=sem, vmem_limit_bytes=VMEM_LIMIT)


def _dot(a, b):
    return jnp.dot(a, b, preferred_element_type=F32)


def _dot_nt(a, b):
    return lax.dot_general(a, b, (((1,), (1,)), ((), ())), preferred_element_type=F32)


def _split3(x):
    hi = x.astype(BF16)
    r = x - hi.astype(F32)
    mid = r.astype(BF16)
    lo = (r - mid.astype(F32)).astype(BF16)
    return hi, mid, lo


def _dot3_right(x, u):
    hi, mid, lo = _split3(x)
    return _dot(hi, u) + _dot(mid, u) + _dot(lo, u)


def _dot2_right(x, u):
    hi = x.astype(BF16)
    lo = (x - hi.astype(F32)).astype(BF16)
    return _dot(hi, u) + _dot(lo, u)


def _dot3_left(u, x):
    hi, mid, lo = _split3(x)
    return _dot(u, hi) + _dot(u, mid) + _dot(u, lo)


def _tri(n, rel):
    r = lax.broadcasted_iota(I32, (n, n), 0)
    c = lax.broadcasted_iota(I32, (n, n), 1)
    m = {"gt": r > c, "lt": r < c, "ge": r >= c, "le": r <= c}[rel]
    return jnp.where(m, 1.0, 0.0).astype(BF16)


def _layer_norm(y, g, b):
    mu = jnp.mean(y, axis=-1, keepdims=True)
    d = y - mu
    var = jnp.mean(d * d, axis=-1, keepdims=True)
    return d * lax.rsqrt(var + LN_EPS) * g + b


def _softplus_tail(z):
    return jnp.log(1.0 + jnp.exp(-jnp.abs(z)))


def _suffix_sums(x, carry):
    u = _tri(LANES, "gt")
    after = carry
    pieces = []
    for c in reversed(range(x.shape[1] // LANES)):
        xc = x[:, c * LANES:(c + 1) * LANES]
        local = _dot3_right(xc, u)
        pieces.append(local + after)
        after = after + local[:, 0:1] + xc[:, 0:1]
    out = pieces[0] if len(pieces) == 1 else jnp.concatenate(pieces[::-1], axis=1)
    return out, after


def _col_to_row(col):
    return jnp.broadcast_to(col, (LANES, LANES)).T[0:1, :]


def _mod_spec(arr, tm, n_grid):
    if arr.shape[1] == 1:
        if n_grid == 2:
            return pl.BlockSpec((1, 1, D_MODEL), lambda g, t: (g, 0, 0))
        return pl.BlockSpec((1, 1, D_MODEL), lambda g, t, k: (g, 0, 0))
    if n_grid == 2:
        return pl.BlockSpec((1, tm, D_MODEL), lambda g, t: (g, t, 0))
    return pl.BlockSpec((1, tm, D_MODEL), lambda g, t, k: (g, t, 0))


def _ada_kernel(c_ref, w_ref, b_ref, o_ref):
    c = c_ref[...]
    a = (c * jax.nn.sigmoid(c)).astype(BF16)
    o_ref[0] = _dot(a, w_ref[0].astype(BF16)) + b_ref[0]


def _ada_mod(c_all, w_ada, b_ada):
    m = c_all.shape[0]
    depth, _, n = w_ada.shape
    tn = 1152
    return pl.pallas_call(
        _ada_kernel,
        grid=(depth, n // tn),
        in_specs=[pl.BlockSpec((m, D_MODEL), lambda l, j: (0, 0)),
                  pl.BlockSpec((1, D_MODEL, tn), lambda l, j: (l, 0, j)),
                  pl.BlockSpec((1, 1, tn), lambda l, j: (l, 0, j))],
        out_specs=pl.BlockSpec((1, m, tn), lambda l, j: (l, 0, j)),
        out_shape=jax.ShapeDtypeStruct((depth, m, n), F32),
        compiler_params=_cparams(("parallel", "parallel")),
        name="ada_mod",
    )(c_all, w_ada, b_ada.reshape(depth, 1, n))


def _ffn_kernel(x_ref, sh_ref, sc_ref, gt_ref, w1_ref, w3_ref, w2_ref, g_ref, b_ref, o_ref,
                h_sc, acc_sc):
    k = pl.program_id(2)

    @pl.when(k == 0)
    def _():
        h_sc[...] = (x_ref[0] * (1.0 + sc_ref[0]) + sh_ref[0]).astype(BF16)
        acc_sc[...] = jnp.zeros_like(acc_sc)

    h = h_sc[...]
    a = _dot(h, w1_ref[...])
    b = _dot(h, w3_ref[...])
    act = (a * jax.nn.sigmoid(a) * b).astype(BF16)
    acc_sc[...] += _dot(act, w2_ref[...])

    @pl.when(k == pl.num_programs(2) - 1)
    def _():
        y = ALPHA * x_ref[0] + gt_ref[0] * (0.5 * acc_sc[...])
        o_ref[0] = _layer_norm(y, g_ref[...], b_ref[...])


def _ffn_sublayer(x, shift, scale, gate, w1, w3, w2, ln_g, ln_b):
    g, tg, _ = x.shape
    tm = min(512, tg)
    tf = D_FF // 2
    xs = pl.BlockSpec((1, tm, D_MODEL), lambda g_, t, k: (g_, t, 0))
    vec = pl.BlockSpec((1, D_MODEL), lambda g_, t, k: (0, 0))
    return pl.pallas_call(
        _ffn_kernel,
        grid=(g, tg // tm, D_FF // tf),
        in_specs=[xs, _mod_spec(shift, tm, 3), _mod_spec(scale, tm, 3), _mod_spec(gate, tm, 3),
                  pl.BlockSpec((D_MODEL, tf), lambda g_, t, k: (0, k)),
                  pl.BlockSpec((D_MODEL, tf), lambda g_, t, k: (0, k)),
                  pl.BlockSpec((tf, D_MODEL), lambda g_, t, k: (k, 0)),
                  vec, vec],
        out_specs=xs,
        out_shape=jax.ShapeDtypeStruct(x.shape, F32),
        scratch_shapes=[pltpu.VMEM((tm, D_MODEL), BF16), pltpu.VMEM((tm, D_MODEL), F32)],
        compiler_params=_cparams(("parallel", "parallel", "arbitrary")),
        name="ffn_sublayer",
    )(x, shift, scale, gate, w1, w3, w2, ln_g.reshape(1, D_MODEL), ln_b.reshape(1, D_MODEL))


def _rotate(x, cos, sin_signed):
    n = x.shape[-1]
    lane = lax.broadcasted_iota(I32, x.shape, 1)
    first = (lane % HEAD_DIM) < (HEAD_DIM // 2)
    other = jnp.where(first, pltpu.roll(x, n - HEAD_DIM // 2, 1), pltpu.roll(x, HEAD_DIM // 2, 1))
    return x * cos[:, :n] + other * sin_signed[:, :n]


def _even_proj_kernel(x_ref, sh_ref, sc_ref, w_ref, cos_ref, sin_ref,
                      kt_ref, vt_ref, kixt_ref, qb_ref, ktb_ref, vb_ref, qix_ref, wix_ref, kixtb_ref):
    h = (x_ref[0] * (1.0 + sc_ref[0]) + sh_ref[0]).astype(BF16)
    cos = cos_ref[...]
    sin = sin_ref[...]

    def proj(lo, width):
        return _dot(h, w_ref[:, lo:lo + width])

    scale = HEAD_DIM ** -0.5 * LOG2E
    q_sb = proj(0, W_HALF)
    qb_ref[0, :, 0:W_HALF] = (q_sb * scale).astype(BF16)
    k_sb = proj(W_HALF, W_HALF).T
    kt_ref[0, 0:W_HALF, :] = k_sb
    ktb_ref[0, 0:W_HALF, :] = k_sb.astype(BF16)
    v_sb = proj(2 * W_HALF, W_HALF)
    vt_ref[0, 0:W_HALF, :] = v_sb.T
    vb_ref[0, :, 0:W_HALF] = v_sb.astype(BF16)
    q_ds = _rotate(proj(3 * W_HALF, W_HALF), cos, sin)
    qb_ref[0, :, W_HALF:2 * W_HALF] = (q_ds * scale).astype(BF16)
    k_ds = _rotate(proj(4 * W_HALF, W_HALF), cos, sin).T
    kt_ref[0, W_HALF:2 * W_HALF, :] = k_ds
    ktb_ref[0, W_HALF:2 * W_HALF, :] = k_ds.astype(BF16)
    v_ds = proj(5 * W_HALF, W_HALF)
    vt_ref[0, W_HALF:2 * W_HALF, :] = v_ds.T
    vb_ref[0, :, W_HALF:2 * W_HALF] = v_ds.astype(BF16)
    ix = proj(IDX_COL, IDX_W)
    ixr = _rotate(ix, cos, sin)
    nq = N_IDX_HEADS * IDX_DIM
    qix_ref[0] = (ixr[:, 0:nq] * IDX_DIM ** -0.5).astype(BF16)
    kix = ixr[:, nq:nq + LANES].T[0:IDX_DIM, :]
    kixt_ref[0] = kix
    kixtb_ref[0] = kix.astype(BF16)
    wix_ref[0] = ix[:, nq + IDX_DIM:nq + IDX_DIM + N_IDX_HEADS] * N_IDX_HEADS ** -0.5


def _even_proj(x, shift, scale, w_pad, cos, sin):
    g, tg, _ = x.shape
    tm = min(512, tg)
    nt = tg // tm
    row = lambda width: pl.BlockSpec((1, tm, width), lambda g_, t: (g_, t, 0))
    col = lambda height: pl.BlockSpec((1, height, tm), lambda g_, t: (g_, 0, t))
    tab = pl.BlockSpec((tm, W_HALF), lambda g_, t: (g_ * nt + t, 0))
    rows = lambda width, dt: jax.ShapeDtypeStruct((g, tg, width), dt)
    cols = lambda height, dt: jax.ShapeDtypeStruct((g, height, tg), dt)
    return pl.pallas_call(
        _even_proj_kernel,
        grid=(g, nt),
        in_specs=[row(D_MODEL), _mod_spec(shift, tm, 2), _mod_spec(scale, tm, 2),
                  pl.BlockSpec((D_MODEL, P_EVEN_PAD), lambda g_, t: (0, 0)), tab, tab],
        out_specs=[col(D_MODEL), col(D_MODEL), col(IDX_DIM), row(D_MODEL), col(D_MODEL), row(D_MODEL),
                   row(N_IDX_HEADS * IDX_DIM), row(N_IDX_HEADS), col(IDX_DIM)],
        out_shape=[cols(D_MODEL, F32), cols(D_MODEL, F32), cols(IDX_DIM, F32), rows(D_MODEL, BF16),
                   cols(D_MODEL, BF16), rows(D_MODEL, BF16), rows(N_IDX_HEADS * IDX_DIM, BF16),
                   rows(N_IDX_HEADS, F32), cols(IDX_DIM, BF16)],
        compiler_params=_cparams(("parallel", "parallel")),
        name="even_proj",
    )(x, shift, scale, w_pad, cos, sin)


def _odd_proj_kernel(x_ref, sh_ref, sc_ref, w_ref, bf_ref,
                     kt_ref, vt_ref, lft_ref, qb_ref, ktb_ref, vb_ref, fq_ref, ft_ref, carry_sc):
    t = pl.program_id(1)
    tm = x_ref.shape[1]
    h = (x_ref[0] * (1.0 + sc_ref[0]) + sh_ref[0]).astype(BF16)

    def proj(lo, width):
        return _dot(h, w_ref[:, lo:lo + width])

    q = proj(0, D_MODEL)
    qb_ref[0] = (q * (HEAD_DIM ** -0.5 * LOG2E)).astype(BF16)
    k = proj(D_MODEL, D_MODEL).T
    kt_ref[0] = k
    ktb_ref[0] = k.astype(BF16)
    v = proj(2 * D_MODEL, D_MODEL)
    vt_ref[0] = v.T
    vb_ref[0] = v.astype(BF16)
    zf = proj(3 * D_MODEL, LANES) + bf_ref[...]
    logf = jnp.minimum(zf, 0.0) - _softplus_tail(zf)
    lft_ref[0] = logf.T[0:N_HEADS, :]

    @pl.when(t == 0)
    def _():
        carry_sc[...] = jnp.zeros_like(carry_sc)

    cum = _dot3_left(_tri(tm, "ge"), logf) + carry_sc[...]
    carry_sc[...] = cum[tm - 1:tm, :]
    cum2 = cum * LOG2E
    fq_ref[0] = cum2[:, 0:N_HEADS]
    ft_ref[0] = cum2.T[0:N_HEADS, :]


def _odd_proj(x, shift, scale, w_pad, bf_pad):
    g, tg, _ = x.shape
    tm = min(512, tg)
    nt = tg // tm
    row = lambda width: pl.BlockSpec((1, tm, width), lambda g_, t: (g_, t, 0))
    col = lambda height: pl.BlockSpec((1, height, tm), lambda g_, t: (g_, 0, t))
    rows = lambda width, dt: jax.ShapeDtypeStruct((g, tg, width), dt)
    cols = lambda height, dt: jax.ShapeDtypeStruct((g, height, tg), dt)
    return pl.pallas_call(
        _odd_proj_kernel,
        grid=(g, nt),
        in_specs=[row(D_MODEL), _mod_spec(shift, tm, 2), _mod_spec(scale, tm, 2),
                  pl.BlockSpec((D_MODEL, P_ODD_PAD), lambda g_, t: (0, 0)),
                  pl.BlockSpec((1, LANES), lambda g_, t: (0, 0))],
        out_specs=[col(D_MODEL), col(D_MODEL), col(N_HEADS), row(D_MODEL), col(D_MODEL), row(D_MODEL),
                   row(N_HEADS), col(N_HEADS)],
        out_shape=[cols(D_MODEL, F32), cols(D_MODEL, F32), cols(N_HEADS, F32), rows(D_MODEL, BF16),
                   cols(D_MODEL, BF16), rows(D_MODEL, BF16), rows(N_HEADS, F32), cols(N_HEADS, F32)],
        scratch_shapes=[pltpu.VMEM((1, LANES), F32)],
        compiler_params=_cparams(("parallel", "arbitrary")),
        name="odd_proj",
    )(x, shift, scale, w_pad, bf_pad)


def _out_proj_kernel(n_parts, *refs):
    o_refs = refs[:n_parts]
    w_refs = refs[n_parts:2 * n_parts]
    x_ref, gt_ref, g_ref, b_ref, y_ref = refs[2 * n_parts:]
    out = _dot(o_refs[0][0], w_refs[0][...])
    for i in range(1, n_parts):
        out = out + _dot(o_refs[i][0], w_refs[i][...])
    y = ALPHA * x_ref[0] + gt_ref[0] * out
    y_ref[0] = _layer_norm(y, g_ref[...], b_ref[...])


def _out_proj(o_parts, w_parts, x, gate, ln_g, ln_b):
    g, tg, _ = x.shape
    tm = min(512, tg)
    n = len(o_parts)
    row = lambda width: pl.BlockSpec((1, tm, width), lambda g_, t: (g_, t, 0))
    vec = pl.BlockSpec((1, D_MODEL), lambda g_, t: (0, 0))
    in_specs = ([row(o.shape[-1]) for o in o_parts]
                + [pl.BlockSpec(w.shape, lambda g_, t: (0, 0)) for w in w_parts]
                + [row(D_MODEL), _mod_spec(gate, tm, 2), vec, vec])
    return pl.pallas_call(
        functools.partial(_out_proj_kernel, n),
        grid=(g, tg // tm),
        in_specs=in_specs,
        out_specs=row(D_MODEL),
        out_shape=jax.ShapeDtypeStruct(x.shape, F32),
        compiler_params=_cparams(("parallel", "parallel")),
        name="out_proj",
    )(*o_parts, *w_parts, x, gate, ln_g.reshape(1, D_MODEL), ln_b.reshape(1, D_MODEL))


def _half_masks(shape):
    lane = lax.broadcasted_iota(I32, shape, 1)
    return lane < HEAD_DIM, lane >= HEAD_DIM


def _sb_kernel(tk, q_ref, kt_ref, v_ref, o_ref):
    tq = q_ref.shape[1]
    i = pl.program_id(2)
    q = q_ref[0]
    qpos = i * tq + lax.broadcasted_iota(I32, (tq, 1), 0)
    halves = _half_masks((tq, LANES))
    qh = [jnp.where(halves[hh], q, jnp.zeros_like(q)) for hh in range(2)]
    kb0 = (i * tq + tq - 1) // tk

    def cond(c):
        return jnp.logical_and(c[0] >= 0, c[1] > 0)

    def body(c):
        kb, _, carries, accs = c
        start = pl.multiple_of(kb * tk, tk)
        kt = kt_ref[0, :, pl.ds(start, tk)]
        v = v_ref[0, pl.ds(start, tk), :]
        vis = (start + lax.broadcasted_iota(I32, (1, tk), 1)) < qpos
        u = _tri(tk, "gt")
        new_carries, new_accs = [], []
        for hh in range(2):
            z = _dot(qh[hh], kt)
            ls = jnp.minimum(z, 0.0) - jnp.log2(1.0 + jnp.exp2(-jnp.abs(z)))
            lm = jnp.where(vis, ls - z, 0.0)
            local = _dot2_right(lm, u)
            w = jnp.where(vis, jnp.exp2(ls + local + carries[hh]), 0.0)
            new_accs.append(accs[hh] + _dot(w.astype(BF16), v))
            new_carries.append(carries[hh] + local[:, 0:1] + lm[:, 0:1])
        go = (jnp.max(jnp.maximum(new_carries[0], new_carries[1])) >= SB_SKIP * LOG2E).astype(I32)
        return kb - 1, go, tuple(new_carries), tuple(new_accs)

    zc = jnp.zeros((tq, 1), F32)
    za = jnp.zeros((tq, LANES), F32)
    accs = lax.while_loop(cond, body, (kb0, jnp.int32(1), (zc, zc), (za, za)))[3]
    o_ref[0] = jnp.where(halves[0], accs[0], accs[1]).astype(o_ref.dtype)


def _sb_attention(qb, kt, vb):
    b, t, _ = qb.shape
    tq, tk = SB_TQ, SB_TK
    return pl.pallas_call(
        functools.partial(_sb_kernel, tk),
        grid=(b, N_SB_PAIRS, t // tq),
        in_specs=[pl.BlockSpec((1, tq, LANES), lambda b_, p, i: (b_, i, p)),
                  pl.BlockSpec((1, LANES, t), lambda b_, p, i: (b_, p, 0)),
                  pl.BlockSpec((1, t, LANES), lambda b_, p, i: (b_, 0, p))],
        out_specs=pl.BlockSpec((1, tq, LANES), lambda b_, p, i: (b_, i, p)),
        out_shape=jax.ShapeDtypeStruct((b, t, W_HALF), BF16),
        compiler_params=_cparams(("parallel", "parallel", "parallel")),
        name="sb_attention",
    )(qb, kt, vb)


def _flash_kernel(mode, tk, *refs):
    if mode == "dsa":
        q_ref, kt_ref, v_ref, sel_ref, o_ref = refs
    else:
        q_ref, kt_ref, v_ref, fq_ref, ft_ref, o_ref = refs
    tq = q_ref.shape[1]
    p = pl.program_id(1)
    i = pl.program_id(2)
    q = q_ref[0]
    qpos = i * tq + lax.broadcasted_iota(I32, (tq, 1), 0)
    n_full = (i * tq) // tk
    n_kb = (i * tq + tq + tk - 1) // tk
    halves = _half_masks((tq, LANES))
    qh = [jnp.where(halves[hh], q, jnp.zeros_like(q)) for hh in range(2)]
    if mode == "fox":
        hl = lax.broadcasted_iota(I32, (tq, N_HEADS), 1)
        fq = [jnp.sum(jnp.where(hl == 2 * p + hh, fq_ref[0], 0.0), axis=1, keepdims=True)
              for hh in range(2)]

    def block(diagonal, kb, c):
        start = pl.multiple_of(kb * tk, tk)
        kt = kt_ref[0, :, pl.ds(start, tk)]
        v = v_ref[0, pl.ds(start, tk), :]
        keep = None
        if diagonal:
            keep = (start + lax.broadcasted_iota(I32, (1, tk), 1)) <= qpos
        if mode == "dsa":
            sel = sel_ref[0, :, pl.ds(start, tk)] > 0
            keep = sel if keep is None else jnp.logical_and(keep, sel)
        out = []
        for hh in range(2):
            m, acc = c[hh]
            s = _dot(qh[hh], kt)
            if mode == "fox":
                s = s + (fq[hh] - ft_ref[0, pl.ds(2 * p + hh, 1), pl.ds(start, tk)])
            if keep is not None:
                s = jnp.where(keep, s, NEG)
            m_new = jnp.maximum(m, jnp.max(s, axis=1, keepdims=True))
            a = jnp.exp2(m - m_new)
            pr = jnp.exp2(s - m_new)
            vh = jnp.where(v_halves[hh], v, jnp.ones_like(v))
            acc = a * acc + _dot(pr.astype(BF16), vh)
            out.append((m_new, acc))
        return tuple(out)

    v_halves = _half_masks((tk, LANES))
    one = (jnp.full((tq, 1), NEG, F32), jnp.zeros((tq, LANES), F32))
    c = lax.fori_loop(0, n_full, functools.partial(block, False), (one, one))
    c = lax.fori_loop(n_full, n_kb, functools.partial(block, True), c)
    acc0, acc1 = c[0][1], c[1][1]
    o_ref[0] = jnp.where(halves[0], acc0 / acc0[:, LANES - 1:LANES], acc1 / acc1[:, 0:1]).astype(o_ref.dtype)


def _flash_attention(mode, qb, kt, vb, extra, pair0, n_pairs):
    b, t, _ = qb.shape
    tq = FLASH_TQ
    tk = min(FLASH_TK, t)
    in_specs = [pl.BlockSpec((1, tq, LANES), lambda b_, p, i: (b_, i, pair0 + p)),
                pl.BlockSpec((1, LANES, t), lambda b_, p, i: (b_, pair0 + p, 0)),
                pl.BlockSpec((1, t, LANES), lambda b_, p, i: (b_, 0, pair0 + p))]
    if mode == "dsa":
        in_specs.append(pl.BlockSpec((1, tq, t), lambda b_, p, i: (b_, i, 0)))
    else:
        in_specs += [pl.BlockSpec((1, tq, N_HEADS), lambda b_, p, i: (b_, i, 0)),
                     pl.BlockSpec((1, N_HEADS, t), lambda b_, p, i: (b_, 0, 0))]
    return pl.pallas_call(
        functools.partial(_flash_kernel, mode, tk),
        grid=(b, n_pairs, t // tq),
        in_specs=in_specs,
        out_specs=pl.BlockSpec((1, tq, LANES), lambda b_, p, i: (b_, i, p)),
        out_shape=jax.ShapeDtypeStruct((b, t, n_pairs * LANES), BF16),
        compiler_params=_cparams(("parallel", "parallel", "parallel")),
        name=mode + "_attention",
    )(qb, kt, vb, *extra)


def _idx_scores(qix, wix, kixt):
    score = None
    for hi in range(N_IDX_HEADS):
        s = _dot(qix[:, hi * IDX_DIM:(hi + 1) * IDX_DIM], kixt)
        term = wix[:, hi:hi + 1] * jnp.maximum(s, 0.0)
        score = term if score is None else score + term
    return score


def _sort_key(score):
    bits = lax.bitcast_convert_type(score, I32)
    sign = bits >> 31
    return ((bits & jnp.int32(0x7FFFFFFF)) ^ sign) - sign


def _select_topk(keys_ref, out_ref, n_chunks, top):
    rows, width = keys_ref.shape
    total_chunks = width // SEL_CHUNK
    sub = SEL_CHUNK // LANES

    def count(pred, cand):
        def body(c, acc):
            base = c * SEL_CHUNK
            for s_ in range(sub):
                kc = keys_ref[:, pl.ds(pl.multiple_of(base + s_ * LANES, LANES), LANES)]
                acc = acc + jnp.where(pred(kc, cand), 1, 0)
            return acc
        acc = lax.fori_loop(0, n_chunks, body, jnp.zeros((rows, LANES), I32))
        return jnp.sum(acc, axis=1, keepdims=True)

    ge = lambda kc, cand: kc >= cand
    zero = jnp.zeros((rows, 1), I32)
    thr = jnp.where(count(ge, zero) >= top, zero, jnp.full((rows, 1), INT_MIN, I32))

    def bit_body(bi, thr):
        cand = thr | (jnp.int32(1) << (30 - bi))
        return jnp.where(count(ge, cand) >= top, cand, thr)

    thr = lax.fori_loop(0, 31, bit_body, thr)
    need = (top - count(lambda kc, cand: kc > cand, thr)).astype(F32)
    u = _tri(LANES, "le")

    def sel_body(c, seen):
        base = c * SEL_CHUNK
        for s_ in range(sub):
            off = pl.multiple_of(base + s_ * LANES, LANES)
            kc = keys_ref[:, pl.ds(off, LANES)]
            is_eq = kc == thr
            eq = jnp.where(is_eq, 1.0, 0.0)
            upto = _dot(eq.astype(BF16), u) + seen
            sel = jnp.logical_or(kc > thr, jnp.logical_and(is_eq, upto - eq < need))
            out_ref[:, pl.ds(off, LANES)] = jnp.where(sel, 1.0, 0.0).astype(out_ref.dtype)
            seen = seen + jnp.sum(eq, axis=1, keepdims=True)
        return seen

    lax.fori_loop(0, n_chunks, sel_body, jnp.zeros((rows, 1), F32))

    def zero_body(c, _):
        off = pl.multiple_of(c * SEL_CHUNK, SEL_CHUNK)
        out_ref[:, pl.ds(off, SEL_CHUNK)] = jnp.zeros((rows, SEL_CHUNK), out_ref.dtype)
        return 0

    lax.fori_loop(n_chunks, total_chunks, zero_body, 0)


def _prompt_select_kernel(top, qix_ref, wix_ref, kixt_ref, sel_ref, keys_sc):
    tq = qix_ref.shape[1]
    i = pl.program_id(1)
    qix = qix_ref[0]
    wix = wix_ref[0]
    qpos = i * tq + lax.broadcasted_iota(I32, (tq, 1), 0)
    n_chunks = (i * tq + tq + SEL_CHUNK - 1) // SEL_CHUNK

    def score_body(c, _):
        start = pl.multiple_of(c * SEL_CHUNK, SEL_CHUNK)
        score = _idx_scores(qix, wix, kixt_ref[0, :, pl.ds(start, SEL_CHUNK)])
        kpos = start + lax.broadcasted_iota(I32, (1, SEL_CHUNK), 1)
        score = jnp.where(kpos <= qpos, score, -jnp.inf)
        keys_sc[:, pl.ds(start, SEL_CHUNK)] = _sort_key(score)
        return 0

    lax.fori_loop(0, n_chunks, score_body, 0)
    _select_topk(keys_sc, sel_ref.at[0], n_chunks, top)


def _prompt_select(qix, wix, kixtb):
    b, t, _ = qix.shape
    tq = SEL_TQ
    top = min(DSA_TOPK, t // 4)
    return pl.pallas_call(
        functools.partial(_prompt_select_kernel, top),
        grid=(b, t // tq),
        in_specs=[pl.BlockSpec((1, tq, N_IDX_HEADS * IDX_DIM), lambda b_, i: (b_, i, 0)),
                  pl.BlockSpec((1, tq, N_IDX_HEADS), lambda b_, i: (b_, i, 0)),
                  pl.BlockSpec((1, IDX_DIM, t), lambda b_, i: (b_, 0, 0))],
        out_specs=pl.BlockSpec((1, tq, t), lambda b_, i: (b_, i, 0)),
        out_shape=jax.ShapeDtypeStruct((b, t, t), BF16),
        scratch_shapes=[pltpu.VMEM((tq, t), I32)],
        compiler_params=_cparams(("parallel", "parallel")),
        name="prompt_select",
    )(qix, wix, kixtb)


def _dec_scores_kernel(n_steps, past, pt_ref, qix_ref, wix_ref, *refs):
    page_refs = refs[:DEC_SCORE_PAGES]
    kn_ref, s_ref = refs[DEC_SCORE_PAGES:]
    j = pl.program_id(1)
    rows = qix_ref.shape[1]
    width = s_ref.shape[2]
    qpos = past + lax.broadcasted_iota(I32, (rows, 1), 0)
    kpos = j * width + lax.broadcasted_iota(I32, (1, width), 1)

    def emit(kixt):
        score = _idx_scores(qix_ref[0], wix_ref[0], kixt.astype(BF16))
        s_ref[0] = jnp.where(kpos <= qpos, score, -jnp.inf)

    @pl.when(j < n_steps)
    def _():
        emit(jnp.concatenate([r[...] for r in page_refs], axis=1))

    @pl.when(j == n_steps)
    def _():
        emit(kn_ref[0])


def _dec_scores(page_table, qix, wix, kix_cache_t, kix_new_t, past):
    db, n_pages = page_table.shape
    rows = qix.shape[1]
    width = DEC_SCORE_PAGES * PAGE
    n_steps = n_pages // DEC_SCORE_PAGES

    def page_spec(k):
        return pl.BlockSpec((None, IDX_DIM, PAGE),
                            lambda b, j, pt: (pt[b, jnp.minimum(j, n_steps - 1) * DEC_SCORE_PAGES + k], 0, 0))

    grid_spec = pltpu.PrefetchScalarGridSpec(
        num_scalar_prefetch=1,
        grid=(db, n_steps + 1),
        in_specs=[pl.BlockSpec((1, rows, N_IDX_HEADS * IDX_DIM), lambda b, j, pt: (b, 0, 0)),
                  pl.BlockSpec((1, rows, N_IDX_HEADS), lambda b, j, pt: (b, 0, 0))]
                 + [page_spec(k) for k in range(DEC_SCORE_PAGES)]
                 + [pl.BlockSpec((1, IDX_DIM, width), lambda b, j, pt: (b, 0, 0))],
        out_specs=pl.BlockSpec((1, rows, width), lambda b, j, pt: (b, 0, j)))
    return pl.pallas_call(
        functools.partial(_dec_scores_kernel, n_steps, past),
        grid_spec=grid_spec,
        out_shape=jax.ShapeDtypeStruct((db, rows, (n_steps + 1) * width), F32),
        compiler_params=_cparams(("parallel", "arbitrary")),
        name="dec_scores",
    )(page_table, qix, wix, *([kix_cache_t] * DEC_SCORE_PAGES), kix_new_t)


def _dec_select_kernel(top, s_ref, sel_ref, keys_sc):
    keys_sc[...] = _sort_key(s_ref[...])
    _select_topk(keys_sc, sel_ref, s_ref.shape[1] // SEL_CHUNK, top)


def _dec_select(scores, top):
    r, width = scores.shape
    tr = min(128, r)
    return pl.pallas_call(
        functools.partial(_dec_select_kernel, top),
        grid=(r // tr,),
        in_specs=[pl.BlockSpec((tr, width), lambda i: (i, 0))],
        out_specs=pl.BlockSpec((tr, width), lambda i: (i, 0)),
        out_shape=jax.ShapeDtypeStruct((r, width), BF16),
        scratch_shapes=[pltpu.VMEM((tr, width), I32)],
        compiler_params=_cparams(("parallel",)),
        name="dec_select",
    )(scores)


def _dec_attn_kernel(mode, n_steps, past, n_q, pt_ref, *refs):
    np_ = DEC_PAGES
    q_ref = refs[0]
    k_refs = refs[1:1 + np_]
    v_refs = refs[1 + np_:1 + 2 * np_]
    kn_ref, vn_ref = refs[1 + 2 * np_:3 + 2 * np_]
    rest = refs[3 + 2 * np_:]
    if mode == "even":
        sel_ref, seln_ref, o_ref, m_sc, l_sc, acc_sc, carry_sc = rest
    else:
        lf_refs = rest[:np_]
        lfn_ref, o_ref, m_sc, l_sc, acc_sc, carry_sc, sq_sc = rest[np_:]
    j = pl.program_id(1)
    n_sb = N_SB_HEADS * n_q if mode == "even" else 0
    row = lax.broadcasted_iota(I32, (DEC_ROWS, 1), 0)
    qpos = past + row % n_q

    @pl.when(j == 0)
    def _():
        m_sc[...] = jnp.full_like(m_sc, NEG)
        l_sc[...] = jnp.zeros_like(l_sc)
        acc_sc[...] = jnp.zeros_like(acc_sc)
        carry_sc[...] = jnp.zeros_like(carry_sc)

    def softmax_rows(s, valid, m_old, l_old):
        s = jnp.where(valid, s, NEG)
        m_new = jnp.maximum(m_old, jnp.max(s, axis=1, keepdims=True))
        a = jnp.exp2(m_old - m_new)
        pr = jnp.where(valid, jnp.exp2(s - m_new), 0.0)
        return pr, a, m_new, a * l_old + jnp.sum(pr, axis=1, keepdims=True)

    def step(kt, vt, key0, extra, first):
        w_ = kt.shape[1]
        s = _dot(q_ref[0], kt.astype(BF16))
        kpos = key0 + lax.broadcasted_iota(I32, (1, w_), 1)
        if mode == "even":
            z = s[0:n_sb]
            vis = kpos < qpos[0:n_sb]
            ls = jnp.minimum(z, 0.0) - jnp.log2(1.0 + jnp.exp2(-jnp.abs(z)))
            lm = jnp.where(vis, ls - z, 0.0)
            rest_, carry = _suffix_sums(lm, carry_sc[...])
            carry_sc[...] = carry
            w = jnp.where(vis, jnp.exp2(ls + rest_), 0.0)
            r_ = lax.broadcasted_iota(I32, (DEC_ROWS - n_sb, DEC_QPAD), 0)
            c_ = lax.broadcasted_iota(I32, (DEC_ROWS - n_sb, DEC_QPAD), 1)
            spread = jnp.where(c_ == r_ % n_q, 1.0, 0.0).astype(BF16)
            selx = _dot(spread, extra) > 0.5
            valid = jnp.logical_and(kpos <= qpos[n_sb:], selx)
            pr, a, m_new, l_new = softmax_rows(s[n_sb:], valid, m_sc[n_sb:], l_sc[n_sb:])
            m_sc[n_sb:] = m_new
            l_sc[n_sb:] = l_new
            p_all = jnp.concatenate([w, pr], axis=0)
            a_all = jnp.concatenate([jnp.ones((n_sb, 1), F32), a], axis=0)
        else:
            suf, carry = _suffix_sums(extra, carry_sc[...])
            carry_sc[...] = carry
            r_ = lax.broadcasted_iota(I32, (DEC_ROWS, N_HEADS), 0)
            c_ = lax.broadcasted_iota(I32, (DEC_ROWS, N_HEADS), 1)
            spread = jnp.where(c_ == r_ // n_q, 1.0, 0.0).astype(BF16)
            suf_rows = _dot3_left(spread, suf)
            if first:
                lane = lax.broadcasted_iota(I32, (1, w_), 1)
                sq_sc[...] = jnp.sum(jnp.where(lane == row % n_q, suf_rows, 0.0), axis=1, keepdims=True)
            logit = s + (suf_rows - sq_sc[...]) * LOG2E
            p_all, a_all, m_new, l_new = softmax_rows(logit, kpos <= qpos, m_sc[...], l_sc[...])
            m_sc[...] = m_new
            l_sc[...] = l_new
        acc_sc[...] = acc_sc[...] * _col_to_row(a_all) + _dot_nt(vt.astype(BF16), p_all.astype(BF16))

    @pl.when(j == 0)
    def _():
        extra = seln_ref[0] if mode == "even" else lfn_ref[0]
        step(kn_ref[0], vn_ref[0], past, extra, True)

    @pl.when(j > 0)
    def _():
        kt = jnp.concatenate([r[...] for r in k_refs], axis=1)
        vt = jnp.concatenate([r[...] for r in v_refs], axis=1)
        if mode == "even":
            extra = sel_ref[0]
        else:
            extra = jnp.concatenate([r[...] for r in lf_refs], axis=1)
        step(kt, vt, past - j * (np_ * PAGE), extra, False)

    @pl.when(j == n_steps)
    def _():
        l = jnp.where(row < n_sb, 1.0, l_sc[...])
        hr = lax.broadcasted_iota(I32, (D_MODEL, DEC_ROWS), 0) // HEAD_DIM
        hc = lax.broadcasted_iota(I32, (D_MODEL, DEC_ROWS), 1) // n_q
        own = jnp.where(hr == hc, acc_sc[...], 0.0).reshape(N_HEADS, HEAD_DIM, DEC_ROWS)
        o_ref[0] = jnp.sum(own, axis=0) / _col_to_row(l)


def _dec_attention(mode, page_table, qbd, k_cache_t, v_cache_t, k_new_t, v_new_t, extra, past, n_q):
    db, n_pages = page_table.shape
    np_ = DEC_PAGES
    n_steps = n_pages // np_
    width = np_ * PAGE
    per_b = lambda b, j, pt: (b, 0, 0)

    def page_spec(k, feat):
        return pl.BlockSpec((None, feat, PAGE),
                            lambda b, j, pt: (pt[b, n_pages - jnp.maximum(j, 1) * np_ + k], 0, 0))

    in_specs = ([pl.BlockSpec((1, DEC_ROWS, D_MODEL), per_b)]
                + [page_spec(k, D_MODEL) for k in range(np_)]
                + [page_spec(k, D_MODEL) for k in range(np_)]
                + [pl.BlockSpec((1, D_MODEL, PAGE), per_b), pl.BlockSpec((1, D_MODEL, PAGE), per_b)])
    scratch = [pltpu.VMEM((DEC_ROWS, 1), F32), pltpu.VMEM((DEC_ROWS, 1), F32),
               pltpu.VMEM((D_MODEL, DEC_ROWS), F32)]
    if mode == "even":
        (sel,) = extra
        rows = sel.shape[1]
        in_specs += [pl.BlockSpec((1, rows, width), lambda b, j, pt: (b, 0, n_steps - jnp.maximum(j, 1))),
                     pl.BlockSpec((1, rows, PAGE), per_b)]
        operands = [sel, sel[:, :, past:past + PAGE]]
        scratch.append(pltpu.VMEM((N_SB_HEADS * n_q, 1), F32))
    else:
        logf_cache_t, logf_new_t = extra
        in_specs += [page_spec(k, N_HEADS) for k in range(np_)]
        in_specs.append(pl.BlockSpec((1, N_HEADS, PAGE), per_b))
        operands = [logf_cache_t] * np_ + [logf_new_t]
        scratch += [pltpu.VMEM((N_HEADS, 1), F32), pltpu.VMEM((DEC_ROWS, 1), F32)]
    grid_spec = pltpu.PrefetchScalarGridSpec(
        num_scalar_prefetch=1,
        grid=(db, n_steps + 1),
        in_specs=in_specs,
        out_specs=pl.BlockSpec((1, HEAD_DIM, DEC_ROWS), per_b),
        scratch_shapes=scratch)
    return pl.pallas_call(
        functools.partial(_dec_attn_kernel, mode, n_steps, past, n_q),
        grid_spec=grid_spec,
        out_shape=jax.ShapeDtypeStruct((db, HEAD_DIM, DEC_ROWS), F32),
        compiler_params=_cparams(("parallel", "arbitrary")),
        name="dec_attention_" + mode,
    )(page_table, qbd, *([k_cache_t] * np_), *([v_cache_t] * np_), k_new_t, v_new_t, *operands)


def _rope_tables(pos):
    half = HEAD_DIM // 2
    inv = ROPE_THETA ** (-jnp.arange(half, dtype=F32) / half)
    ang = pos.astype(F32)[:, None] * inv
    cos, sin = jnp.cos(ang), jnp.sin(ang)
    reps = W_HALF // HEAD_DIM
    return (jnp.tile(jnp.concatenate([cos, cos], axis=-1), (1, reps)),
            jnp.tile(jnp.concatenate([-sin, sin], axis=-1), (1, reps)))


def _heads_view(xt):
    g, _, t = xt.shape
    return xt.reshape(g, N_HEADS, HEAD_DIM, t).transpose(0, 3, 1, 2)


def _new_block(xt, db, n_q, width):
    f = xt.shape[0]
    return jnp.pad(xt.reshape(f, db, n_q).transpose(1, 0, 2), ((0, 0), (0, 0), (0, width - n_q)))


def _block_diag_queries(qb, db, n_q):
    q = qb.reshape(db, n_q, N_HEADS, HEAD_DIM).transpose(0, 2, 1, 3)
    eye = jnp.eye(N_HEADS, dtype=qb.dtype)
    q = q[:, :, :, None, :] * eye[None, :, None, :, None]
    q = q.reshape(db, N_HEADS * n_q, D_MODEL)
    return jnp.pad(q, ((0, 0), (0, DEC_ROWS - N_HEADS * n_q), (0, 0)))


def _rows_token(o, db, n_q):
    o = o[:, :, :N_HEADS * n_q].reshape(db, HEAD_DIM, N_HEADS, n_q)
    return o.transpose(0, 3, 2, 1).reshape(1, db * n_q, D_MODEL)


def kernel(x_prompt, x_sample, cache_k_even, cache_v_even, cache_kidx_even, cache_k_odd, cache_v_odd,
           cache_logf_odd, page_table, c_prompt, c_sample, w_ada, b_ada, ln_g, ln_b, ffn_w1, ffn_w3,
           ffn_w2, w_in_even, w_out_even, w_in_odd, b_forget, w_out_odd):
    b, t, _ = x_prompt.shape
    db, n_q, _ = x_sample.shape
    n_pages = page_table.shape[1]
    past = n_pages * PAGE
    pool = cache_k_even.shape[1]

    n_c = b + db
    c_all = jnp.pad(jnp.concatenate([c_prompt, c_sample], axis=0), ((0, (-n_c) % 8), (0, 0)))
    mod = _ada_mod(c_all, w_ada, b_ada).reshape(DEPTH, -1, N_SUB, 3, D_MODEL)

    def mods(layer, sub):
        m = mod[layer, :, sub]
        mp = [m[:b, i].reshape(b, 1, D_MODEL) for i in range(3)]
        ms = [jnp.repeat(m[b:n_c, i], n_q, axis=0).reshape(1, db * n_q, D_MODEL) for i in range(3)]
        return mp, ms

    cos_p, sin_p = _rope_tables(jnp.tile(jnp.arange(t, dtype=I32), b))
    cos_s, sin_s = _rope_tables(jnp.tile(past + jnp.arange(n_q, dtype=I32), db))

    w_even = jnp.pad(w_in_even, ((0, 0), (0, 0), (0, P_EVEN_PAD - P_EVEN))).astype(BF16)
    w_odd = jnp.pad(w_in_odd, ((0, 0), (0, 0), (0, P_ODD_PAD - w_in_odd.shape[-1]))).astype(BF16)
    bf_pad = jnp.pad(b_forget, ((0, 0), (0, LANES - N_HEADS)))
    w1, w3, w2 = ffn_w1.astype(BF16), ffn_w3.astype(BF16), ffn_w2.astype(BF16)
    wo_even, wo_odd = w_out_even.astype(BF16), w_out_odd.astype(BF16)

    def cache_t(cache, j, feat):
        c = cache[j].reshape(pool, PAGE, feat)
        return c.transpose(0, 2, 1)

    top_s = min(DSA_TOPK, (past + n_q) // 4)
    score_w = DEC_SCORE_PAGES * PAGE

    xp = x_prompt
    xs = x_sample.reshape(1, db * n_q, D_MODEL)
    even_p, odd_p, even_s, odd_s = [], [], [], []
    for layer in range(DEPTH):
        j = layer // 2
        (mp0, ms0), (mp1, ms1), (mp2, ms2) = mods(layer, 0), mods(layer, 1), mods(layer, 2)
        ffn_a = (w1[layer, 0], w3[layer, 0], w2[layer, 0], ln_g[layer, 0], ln_b[layer, 0])
        xp = _ffn_sublayer(xp, *mp0, *ffn_a)
        xs = _ffn_sublayer(xs, *ms0, *ffn_a)
        if layer % 2 == 0:
            kt32, vt32, kixt32, qb, ktb, vb, qix, wix, kixtb = _even_proj(
                xp, mp1[0], mp1[1], w_even[j], cos_p, sin_p)
            even_p.append((_heads_view(kt32), _heads_view(vt32), kixt32.transpose(0, 2, 1)))
            o_sb = _sb_attention(qb, ktb, vb)
            sel = _prompt_select(qix, wix, kixtb)
            o_ds = _flash_attention("dsa", qb, ktb, vb, (sel,), N_SB_PAIRS, N_PAIRS - N_SB_PAIRS)
            xp = _out_proj((o_sb, o_ds), (wo_even[j, :W_HALF], wo_even[j, W_HALF:]), xp, mp1[2],
                           ln_g[layer, 1], ln_b[layer, 1])
            kt32, vt32, kixt32, qb, _, _, qix, wix, _ = _even_proj(xs, ms1[0], ms1[1], w_even[j], cos_s, sin_s)
            even_s.append((kt32[0].T.reshape(db, n_q, N_HEADS, HEAD_DIM),
                           vt32[0].T.reshape(db, n_q, N_HEADS, HEAD_DIM),
                           kixt32[0].T.reshape(db, n_q, IDX_DIM)))
            pad_q = lambda a: jnp.pad(a.reshape(db, n_q, -1), ((0, 0), (0, DEC_QPAD - n_q), (0, 0)))
            scores = _dec_scores(page_table, pad_q(qix), pad_q(wix), cache_t(cache_kidx_even, j, IDX_DIM),
                                 _new_block(kixt32[0], db, n_q, score_w), past)
            sel = _dec_select(scores.reshape(db * DEC_QPAD, -1), top_s).reshape(db, DEC_QPAD, -1)
            o = _dec_attention(
                "even", page_table, _block_diag_queries(qb[0], db, n_q),
                cache_t(cache_k_even, j, D_MODEL), cache_t(cache_v_even, j, D_MODEL),
                _new_block(kt32[0], db, n_q, PAGE), _new_block(vt32[0], db, n_q, PAGE),
                (sel,), past, n_q)
            xs = _out_proj((_rows_token(o, db, n_q).astype(BF16),), (wo_even[j],), xs, ms1[2],
                           ln_g[layer, 1], ln_b[layer, 1])
        else:
            kt32, vt32, lft, qb, ktb, vb, fq, ft = _odd_proj(xp, mp1[0], mp1[1], w_odd[j], bf_pad[j:j + 1])
            odd_p.append((_heads_view(kt32), _heads_view(vt32), lft.transpose(0, 2, 1)))
            o = _flash_attention("fox", qb, ktb, vb, (fq, ft), 0, N_PAIRS)
            xp = _out_proj((o,), (wo_odd[j],), xp, mp1[2], ln_g[layer, 1], ln_b[layer, 1])
            kt32, vt32, lft, qb, _, _, _, _ = _odd_proj(xs, ms1[0], ms1[1], w_odd[j], bf_pad[j:j + 1])
            odd_s.append((kt32[0].T.reshape(db, n_q, N_HEADS, HEAD_DIM),
                          vt32[0].T.reshape(db, n_q, N_HEADS, HEAD_DIM),
                          lft[0].T.reshape(db, n_q, N_HEADS)))
            o = _dec_attention(
                "odd", page_table, _block_diag_queries(qb[0], db, n_q),
                cache_t(cache_k_odd, j, D_MODEL), cache_t(cache_v_odd, j, D_MODEL),
                _new_block(kt32[0], db, n_q, PAGE), _new_block(vt32[0], db, n_q, PAGE),
                (cache_t(cache_logf_odd, j, N_HEADS), _new_block(lft[0], db, n_q, PAGE)), past, n_q)
            xs = _out_proj((_rows_token(o, db, n_q).astype(BF16),), (wo_odd[j],), xs, ms1[2],
                           ln_g[layer, 1], ln_b[layer, 1])
        ffn_b = (w1[layer, 1], w3[layer, 1], w2[layer, 1], ln_g[layer, 2], ln_b[layer, 2])
        xp = _ffn_sublayer(xp, *mp2, *ffn_b)
        xs = _ffn_sublayer(xs, *ms2, *ffn_b)

    def stack(rows, i):
        return jnp.stack([r[i] for r in rows])

    return (xp, xs.reshape(db, n_q, D_MODEL),
            stack(even_p, 0), stack(even_p, 1), stack(even_p, 2),
            stack(odd_p, 0), stack(odd_p, 1), stack(odd_p, 2),
            stack(even_s, 0), stack(even_s, 1), stack(even_s, 2),
            stack(odd_s, 0), stack(odd_s, 1), stack(odd_s, 2))
```

```python
import functools
import math

import jax
import jax.numpy as jnp
from jax import lax
from jax.experimental import pallas as pl
from jax.experimental.pallas import tpu as pltpu

F32 = jnp.float32
BF16 = jnp.bfloat16
I32 = jnp.int32

D_MODEL = 1024
HEAD_DIM = 64
N_HEADS = 16
N_SB_HEADS = 8
N_PAIRS = N_HEADS // 2
N_SB_PAIRS = N_SB_HEADS // 2
W_HALF = N_SB_HEADS * HEAD_DIM
N_IDX_HEADS = 4
IDX_DIM = 64
DSA_TOPK = 256
D_FF = 2816
ROPE_THETA = 10000.0
PAGE = 128
N_SUB = 3
DEPTH = 2
ALPHA = (2 * DEPTH) ** 0.25
LN_EPS = 1e-5
LANES = 128
P_EVEN = 6 * W_HALF + N_IDX_HEADS * IDX_DIM + IDX_DIM + N_IDX_HEADS
P_EVEN_PAD = 3456
IDX_COL = 6 * W_HALF
IDX_W = P_EVEN_PAD - IDX_COL
P_ODD_PAD = 3 * D_MODEL + LANES
NEG = -1e30
SB_SKIP = -105.0
INT_MIN = -(2 ** 31)
LOG2E = math.log2(math.e)
SEL_CHUNK = 512
DEC_ROWS = 128
SEL_TQ = 128
SB_TQ = 512
SB_TK = 256
FLASH_TQ = 1024
FLASH_TK = 1024
DEC_QPAD = 16
DEC_PAGES = 8
DEC_SCORE_PAGES = 8
VMEM_LIMIT = 56 * 1024 * 1024


def _cparams(sem):
    return pltpu.CompilerParams(dimension_semantics=sem, vmem_limit_bytes=VMEM_LIMIT)


def _dot(a, b):
    return jnp.dot(a, b, preferred_element_type=F32)


def _dot_nt(a, b):
    return lax.dot_general(a, b, (((1,), (1,)), ((), ())), preferred_element_type=F32)


def _split3(x):
    hi = x.astype(BF16)
    r = x - hi.astype(F32)
    mid = r.astype(BF16)
    lo = (r - mid.astype(F32)).astype(BF16)
    return hi, mid, lo


def _pieces(x):
    return tuple(p.astype(F32) for p in _split3(x))


_PIECE_PAIRS = ((0, 0), (1, 1), (0, 2), (1, 2))
_PIECE_ROWS = (None, 1, 0, 1, 2, 0, 2, 1)


def _dot3_right(x, u):
    hi, mid, lo = _split3(x)
    return _dot(hi, u) + _dot(mid, u) + _dot(lo, u)


def _dot2_right(x, u):
    hi = x.astype(BF16)
    lo = (x - hi.astype(F32)).astype(BF16)
    return _dot(hi, u) + _dot(lo, u)


def _dot3_left(u, x):
    hi, mid, lo = _split3(x)
    return _dot(u, hi) + _dot(u, mid) + _dot(u, lo)


def _tri(n, rel):
    r = lax.broadcasted_iota(I32, (n, n), 0)
    c = lax.broadcasted_iota(I32, (n, n), 1)
    m = {"gt": r > c, "lt": r < c, "ge": r >= c, "le": r <= c}[rel]
    return jnp.where(m, 1.0, 0.0).astype(BF16)


def _layer_norm(y, g, b):
    mu = jnp.mean(y, axis=-1, keepdims=True)
    d = y - mu
    var = jnp.mean(d * d, axis=-1, keepdims=True)
    return d * lax.rsqrt(var + LN_EPS) * g + b


def _softplus_tail(z):
    return jnp.log(1.0 + jnp.exp(-jnp.abs(z)))


def _suffix_sums(x, carry):
    u = _tri(LANES, "gt")
    after = carry
    pieces = []
    for c in reversed(range(x.shape[1] // LANES)):
        xc = x[:, c * LANES:(c + 1) * LANES]
        local = _dot3_right(xc, u)
        pieces.append(local + after)
        after = after + local[:, 0:1] + xc[:, 0:1]
    out = pieces[0] if len(pieces) == 1 else jnp.concatenate(pieces[::-1], axis=1)
    return out, after


def _col_to_row(col):
    return jnp.broadcast_to(col, (LANES, LANES)).T[0:1, :]


def _mod_spec(arr, tm, n_grid):
    if arr.shape[1] == 1:
        if n_grid == 2:
            return pl.BlockSpec((1, 1, D_MODEL), lambda g, t: (g, 0, 0))
        return pl.BlockSpec((1, 1, D_MODEL), lambda g, t, k: (g, 0, 0))
    if n_grid == 2:
        return pl.BlockSpec((1, tm, D_MODEL), lambda g, t: (g, t, 0))
    return pl.BlockSpec((1, tm, D_MODEL), lambda g, t, k: (g, t, 0))


def _ada_kernel(c_ref, w_ref, b_ref, o_ref):
    c = c_ref[...]
    a = (c * jax.nn.sigmoid(c)).astype(BF16)
    o_ref[0] = _dot(a, w_ref[0].astype(BF16)) + b_ref[0]


def _ada_mod(c_all, w_ada, b_ada):
    m = c_all.shape[0]
    depth, _, n = w_ada.shape
    tn = 1152
    return pl.pallas_call(
        _ada_kernel,
        grid=(depth, n // tn),
        in_specs=[pl.BlockSpec((m, D_MODEL), lambda l, j: (0, 0)),
                  pl.BlockSpec((1, D_MODEL, tn), lambda l, j: (l, 0, j)),
                  pl.BlockSpec((1, 1, tn), lambda l, j: (l, 0, j))],
        out_specs=pl.BlockSpec((1, m, tn), lambda l, j: (l, 0, j)),
        out_shape=jax.ShapeDtypeStruct((depth, m, n), F32),
        compiler_params=_cparams(("parallel", "parallel")),
        name="ada_mod",
    )(c_all, w_ada, b_ada.reshape(depth, 1, n))


def _ffn_kernel(x_ref, sh_ref, sc_ref, gt_ref, w1_ref, w3_ref, w2_ref, g_ref, b_ref, o_ref,
                h_sc, acc_sc):
    k = pl.program_id(2)

    @pl.when(k == 0)
    def _():
        h_sc[...] = (x_ref[0] * (1.0 + sc_ref[0]) + sh_ref[0]).astype(BF16)
        acc_sc[...] = jnp.zeros_like(acc_sc)

    h = h_sc[...]
    a = _dot(h, w1_ref[...])
    b = _dot(h, w3_ref[...])
    act = (a * jax.nn.sigmoid(a) * b).astype(BF16)
    acc_sc[...] += _dot(act, w2_ref[...])

    @pl.when(k == pl.num_programs(2) - 1)
    def _():
        y = ALPHA * x_ref[0] + gt_ref[0] * (0.5 * acc_sc[...])
        o_ref[0] = _layer_norm(y, g_ref[...], b_ref[...])


def _ffn_sublayer(x, shift, scale, gate, w1, w3, w2, ln_g, ln_b):
    g, tg, _ = x.shape
    tm = min(512, tg)
    tf = D_FF // 2
    xs = pl.BlockSpec((1, tm, D_MODEL), lambda g_, t, k: (g_, t, 0))
    vec = pl.BlockSpec((1, D_MODEL), lambda g_, t, k: (0, 0))
    return pl.pallas_call(
        _ffn_kernel,
        grid=(g, tg // tm, D_FF // tf),
        in_specs=[xs, _mod_spec(shift, tm, 3), _mod_spec(scale, tm, 3), _mod_spec(gate, tm, 3),
                  pl.BlockSpec((D_MODEL, tf), lambda g_, t, k: (0, k)),
                  pl.BlockSpec((D_MODEL, tf), lambda g_, t, k: (0, k)),
                  pl.BlockSpec((tf, D_MODEL), lambda g_, t, k: (k, 0)),
                  vec, vec],
        out_specs=xs,
        out_shape=jax.ShapeDtypeStruct(x.shape, F32),
        scratch_shapes=[pltpu.VMEM((tm, D_MODEL), BF16), pltpu.VMEM((tm, D_MODEL), F32)],
        compiler_params=_cparams(("parallel", "parallel", "arbitrary")),
        name="ffn_sublayer",
    )(x, shift, scale, gate, w1, w3, w2, ln_g.reshape(1, D_MODEL), ln_b.reshape(1, D_MODEL))


def _rotate(x, cos, sin_signed):
    n = x.shape[-1]
    lane = lax.broadcasted_iota(I32, x.shape, 1)
    first = (lane % HEAD_DIM) < (HEAD_DIM // 2)
    other = jnp.where(first, pltpu.roll(x, n - HEAD_DIM // 2, 1), pltpu.roll(x, HEAD_DIM // 2, 1))
    return x * cos[:, :n] + other * sin_signed[:, :n]


def _even_proj_kernel(x_ref, sh_ref, sc_ref, w_ref, cos_ref, sin_ref,
                      kt_ref, vt_ref, kixt_ref, qb_ref, ktb_ref, vb_ref, qix_ref, wix_ref, kixb_ref,
                      qixt_ref, wixt_ref):
    h = (x_ref[0] * (1.0 + sc_ref[0]) + sh_ref[0]).astype(BF16)
    cos = cos_ref[...]
    sin = sin_ref[...]

    def proj(lo, width):
        return _dot(h, w_ref[:, lo:lo + width])

    scale = HEAD_DIM ** -0.5 * LOG2E
    q_sb = proj(0, W_HALF)
    qb_ref[0, :, 0:W_HALF] = (q_sb * scale).astype(BF16)
    k_sb = proj(W_HALF, W_HALF).T
    kt_ref[0, 0:W_HALF, :] = k_sb
    ktb_ref[0, 0:W_HALF, :] = k_sb.astype(BF16)
    v_sb = proj(2 * W_HALF, W_HALF)
    vt_ref[0, 0:W_HALF, :] = v_sb.T
    vb_ref[0, :, 0:W_HALF] = v_sb.astype(BF16)
    q_ds = _rotate(proj(3 * W_HALF, W_HALF), cos, sin)
    qb_ref[0, :, W_HALF:2 * W_HALF] = (q_ds * scale).astype(BF16)
    k_ds = _rotate(proj(4 * W_HALF, W_HALF), cos, sin).T
    kt_ref[0, W_HALF:2 * W_HALF, :] = k_ds
    ktb_ref[0, W_HALF:2 * W_HALF, :] = k_ds.astype(BF16)
    v_ds = proj(5 * W_HALF, W_HALF)
    vt_ref[0, W_HALF:2 * W_HALF, :] = v_ds.T
    vb_ref[0, :, W_HALF:2 * W_HALF] = v_ds.astype(BF16)
    ix = proj(IDX_COL, IDX_W)
    ixr = _rotate(ix, cos, sin)
    nq = N_IDX_HEADS * IDX_DIM
    q_ix = ixr[:, 0:nq] * IDX_DIM ** -0.5
    qp = _pieces(q_ix)
    for n in range(3):
        qix_ref[0, :, n * nq:(n + 1) * nq] = qp[n].astype(BF16)
        qixt_ref[0, n * nq:(n + 1) * nq, :] = qp[n].T.astype(BF16)
    k_ix = ixr[:, nq:nq + LANES]
    kixt_ref[0] = k_ix.T[0:IDX_DIM, :]
    kp = _pieces(k_ix)
    low = lax.broadcasted_iota(I32, k_ix.shape, 1) < IDX_DIM
    side = lambda a, b: jnp.where(low, a, pltpu.roll(b, IDX_DIM, 1)).astype(BF16)
    for n, (a, b) in enumerate(_PIECE_PAIRS):
        kixb_ref[0, :, n * LANES:(n + 1) * LANES] = side(kp[a], kp[b])
    wix_ref[0] = ix[:, nq + IDX_DIM:nq + IDX_DIM + N_IDX_HEADS] * N_IDX_HEADS ** -0.5
    wixt_ref[0] = ix[:, nq:nq + LANES].T[IDX_DIM:IDX_DIM + 8, :] * N_IDX_HEADS ** -0.5


def _even_proj(x, shift, scale, w_pad, cos, sin):
    g, tg, _ = x.shape
    tm = min(512, tg)
    nt = tg // tm
    row = lambda width: pl.BlockSpec((1, tm, width), lambda g_, t: (g_, t, 0))
    col = lambda height: pl.BlockSpec((1, height, tm), lambda g_, t: (g_, 0, t))
    tab = pl.BlockSpec((tm, W_HALF), lambda g_, t: (g_ * nt + t, 0))
    rows = lambda width, dt: jax.ShapeDtypeStruct((g, tg, width), dt)
    cols = lambda height, dt: jax.ShapeDtypeStruct((g, height, tg), dt)
    return pl.pallas_call(
        _even_proj_kernel,
        grid=(g, nt),
        in_specs=[row(D_MODEL), _mod_spec(shift, tm, 2), _mod_spec(scale, tm, 2),
                  pl.BlockSpec((D_MODEL, P_EVEN_PAD), lambda g_, t: (0, 0)), tab, tab],
        out_specs=[col(D_MODEL), col(D_MODEL), col(IDX_DIM), row(D_MODEL), col(D_MODEL), row(D_MODEL),
                   row(3 * N_IDX_HEADS * IDX_DIM), row(N_IDX_HEADS), row(4 * LANES),
                   col(3 * N_IDX_HEADS * IDX_DIM), col(8)],
        out_shape=[cols(D_MODEL, F32), cols(D_MODEL, F32), cols(IDX_DIM, F32), rows(D_MODEL, BF16),
                   cols(D_MODEL, BF16), rows(D_MODEL, BF16), rows(3 * N_IDX_HEADS * IDX_DIM, BF16),
                   rows(N_IDX_HEADS, F32), rows(4 * LANES, BF16),
                   cols(3 * N_IDX_HEADS * IDX_DIM, BF16), cols(8, F32)],
        compiler_params=_cparams(("parallel", "parallel")),
        name="even_proj",
    )(x, shift, scale, w_pad, cos, sin)


def _odd_proj_kernel(x_ref, sh_ref, sc_ref, w_ref, bf_ref,
                     kt_ref, vt_ref, lft_ref, qb_ref, ktb_ref, vb_ref, fq_ref, ft_ref, carry_sc):
    t = pl.program_id(1)
    tm = x_ref.shape[1]
    h = (x_ref[0] * (1.0 + sc_ref[0]) + sh_ref[0]).astype(BF16)

    def proj(lo, width):
        return _dot(h, w_ref[:, lo:lo + width])

    q = proj(0, D_MODEL)
    qb_ref[0] = (q * (HEAD_DIM ** -0.5 * LOG2E)).astype(BF16)
    k = proj(D_MODEL, D_MODEL).T
    kt_ref[0] = k
    ktb_ref[0] = k.astype(BF16)
    v = proj(2 * D_MODEL, D_MODEL)
    vt_ref[0] = v.T
    vb_ref[0] = v.astype(BF16)
    zf = proj(3 * D_MODEL, LANES) + bf_ref[...]
    logf = jnp.minimum(zf, 0.0) - _softplus_tail(zf)
    lft_ref[0] = logf.T[0:N_HEADS, :]

    @pl.when(t == 0)
    def _():
        carry_sc[...] = jnp.zeros_like(carry_sc)

    cum = _dot3_left(_tri(tm, "ge"), logf) + carry_sc[...]
    carry_sc[...] = cum[tm - 1:tm, :]
    cum2 = cum * LOG2E
    fq_ref[0] = cum2[:, 0:N_HEADS]
    ft_ref[0] = cum2.T[0:N_HEADS, :]


def _odd_proj(x, shift, scale, w_pad, bf_pad):
    g, tg, _ = x.shape
    tm = min(512, tg)
    nt = tg // tm
    row = lambda width: pl.BlockSpec((1, tm, width), lambda g_, t: (g_, t, 0))
    col = lambda height: pl.BlockSpec((1, height, tm), lambda g_, t: (g_, 0, t))
    rows = lambda width, dt: jax.ShapeDtypeStruct((g, tg, width), dt)
    cols = lambda height, dt: jax.ShapeDtypeStruct((g, height, tg), dt)
    return pl.pallas_call(
        _odd_proj_kernel,
        grid=(g, nt),
        in_specs=[row(D_MODEL), _mod_spec(shift, tm, 2), _mod_spec(scale, tm, 2),
                  pl.BlockSpec((D_MODEL, P_ODD_PAD), lambda g_, t: (0, 0)),
                  pl.BlockSpec((1, LANES), lambda g_, t: (0, 0))],
        out_specs=[col(D_MODEL), col(D_MODEL), col(N_HEADS), row(D_MODEL), col(D_MODEL), row(D_MODEL),
                   row(N_HEADS), col(N_HEADS)],
        out_shape=[cols(D_MODEL, F32), cols(D_MODEL, F32), cols(N_HEADS, F32), rows(D_MODEL, BF16),
                   cols(D_MODEL, BF16), rows(D_MODEL, BF16), rows(N_HEADS, F32), cols(N_HEADS, F32)],
        scratch_shapes=[pltpu.VMEM((1, LANES), F32)],
        compiler_params=_cparams(("parallel", "arbitrary")),
        name="odd_proj",
    )(x, shift, scale, w_pad, bf_pad)


def _out_proj_kernel(n_parts, *refs):
    o_refs = refs[:n_parts]
    w_refs = refs[n_parts:2 * n_parts]
    x_ref, gt_ref, g_ref, b_ref, y_ref = refs[2 * n_parts:]
    out = _dot(o_refs[0][0], w_refs[0][...])
    for i in range(1, n_parts):
        out = out + _dot(o_refs[i][0], w_refs[i][...])
    y = ALPHA * x_ref[0] + gt_ref[0] * out
    y_ref[0] = _layer_norm(y, g_ref[...], b_ref[...])


def _out_proj(o_parts, w_parts, x, gate, ln_g, ln_b):
    g, tg, _ = x.shape
    tm = min(512, tg)
    n = len(o_parts)
    row = lambda width: pl.BlockSpec((1, tm, width), lambda g_, t: (g_, t, 0))
    vec = pl.BlockSpec((1, D_MODEL), lambda g_, t: (0, 0))
    in_specs = ([row(o.shape[-1]) for o in o_parts]
                + [pl.BlockSpec(w.shape, lambda g_, t: (0, 0)) for w in w_parts]
                + [row(D_MODEL), _mod_spec(gate, tm, 2), vec, vec])
    return pl.pallas_call(
        functools.partial(_out_proj_kernel, n),
        grid=(g, tg // tm),
        in_specs=in_specs,
        out_specs=row(D_MODEL),
        out_shape=jax.ShapeDtypeStruct(x.shape, F32),
        compiler_params=_cparams(("parallel", "parallel")),
        name="out_proj",
    )(*o_parts, *w_parts, x, gate, ln_g.reshape(1, D_MODEL), ln_b.reshape(1, D_MODEL))


def _half_masks(shape):
    lane = lax.broadcasted_iota(I32, shape, 1)
    return lane < HEAD_DIM, lane >= HEAD_DIM


def _sb_kernel(tk, q_ref, kt_ref, v_ref, o_ref):
    tq = q_ref.shape[1]
    i = pl.program_id(2)
    q = q_ref[0]
    qpos = i * tq + lax.broadcasted_iota(I32, (tq, 1), 0)
    halves = _half_masks((tq, LANES))
    qh = [jnp.where(halves[hh], q, jnp.zeros_like(q)) for hh in range(2)]
    kb0 = (i * tq + tq - 1) // tk

    def cond(c):
        return jnp.logical_and(c[0] >= 0, c[1] > 0)

    def body(c):
        kb, _, carries, accs = c
        start = pl.multiple_of(kb * tk, tk)
        kt = kt_ref[0, :, pl.ds(start, tk)]
        v = v_ref[0, pl.ds(start, tk), :]
        vis = (start + lax.broadcasted_iota(I32, (1, tk), 1)) < qpos
        u = _tri(tk, "gt")
        new_carries, new_accs = [], []
        for hh in range(2):
            z = _dot(qh[hh], kt)
            ls = jnp.minimum(z, 0.0) - jnp.log2(1.0 + jnp.exp2(-jnp.abs(z)))
            lm = jnp.where(vis, ls - z, 0.0)
            local = _dot2_right(lm, u)
            w = jnp.where(vis, jnp.exp2(ls + local + carries[hh]), 0.0)
            new_accs.append(accs[hh] + _dot(w.astype(BF16), v))
            new_carries.append(carries[hh] + local[:, 0:1] + lm[:, 0:1])
        go = (jnp.max(jnp.maximum(new_carries[0], new_carries[1])) >= SB_SKIP * LOG2E).astype(I32)
        return kb - 1, go, tuple(new_carries), tuple(new_accs)

    zc = jnp.zeros((tq, 1), F32)
    za = jnp.zeros((tq, LANES), F32)
    accs = lax.while_loop(cond, body, (kb0, jnp.int32(1), (zc, zc), (za, za)))[3]
    o_ref[0] = jnp.where(halves[0], accs[0], accs[1]).astype(o_ref.dtype)


def _sb_attention(qb, kt, vb):
    b, t, _ = qb.shape
    tq, tk = SB_TQ, SB_TK
    return pl.pallas_call(
        functools.partial(_sb_kernel, tk),
        grid=(b, N_SB_PAIRS, t // tq),
        in_specs=[pl.BlockSpec((1, tq, LANES), lambda b_, p, i: (b_, i, p)),
                  pl.BlockSpec((1, LANES, t), lambda b_, p, i: (b_, p, 0)),
                  pl.BlockSpec((1, t, LANES), lambda b_, p, i: (b_, 0, p))],
        out_specs=pl.BlockSpec((1, tq, LANES), lambda b_, p, i: (b_, i, p)),
        out_shape=jax.ShapeDtypeStruct((b, t, W_HALF), BF16),
        compiler_params=_cparams(("parallel", "parallel", "parallel")),
        name="sb_attention",
    )(qb, kt, vb)


def _flash_kernel(mode, tk, *refs):
    if mode == "dsa":
        q_ref, kt_ref, v_ref, sel_ref, o_ref = refs
    else:
        q_ref, kt_ref, v_ref, fq_ref, ft_ref, o_ref = refs
    tq = q_ref.shape[1]
    p = pl.program_id(1)
    i = pl.program_id(2)
    q = q_ref[0]
    qpos = i * tq + lax.broadcasted_iota(I32, (tq, 1), 0)
    n_full = (i * tq) // tk
    n_kb = (i * tq + tq + tk - 1) // tk
    halves = _half_masks((tq, LANES))
    qh = [jnp.where(halves[hh], q, jnp.zeros_like(q)) for hh in range(2)]
    if mode == "fox":
        hl = lax.broadcasted_iota(I32, (tq, N_HEADS), 1)
        fq = [jnp.sum(jnp.where(hl == 2 * p + hh, fq_ref[0], 0.0), axis=1, keepdims=True)
              for hh in range(2)]

    def block(diagonal, kb, c):
        start = pl.multiple_of(kb * tk, tk)
        kt = kt_ref[0, :, pl.ds(start, tk)]
        v = v_ref[0, pl.ds(start, tk), :]
        keep = None
        if diagonal:
            keep = (start + lax.broadcasted_iota(I32, (1, tk), 1)) <= qpos
        if mode == "dsa":
            sel = sel_ref[0, :, pl.ds(start, tk)] > 0
            keep = sel if keep is None else jnp.logical_and(keep, sel)
        out = []
        for hh in range(2):
            m, acc = c[hh]
            s = _dot(qh[hh], kt)
            if mode == "fox":
                s = s + (fq[hh] - ft_ref[0, pl.ds(2 * p + hh, 1), pl.ds(start, tk)])
            if keep is not None:
                s = jnp.where(keep, s, NEG)
            m_new = jnp.maximum(m, jnp.max(s, axis=1, keepdims=True))
            a = jnp.exp2(m - m_new)
            pr = jnp.exp2(s - m_new)
            vh = jnp.where(v_halves[hh], v, jnp.ones_like(v))
            acc = a * acc + _dot(pr.astype(BF16), vh)
            out.append((m_new, acc))
        return tuple(out)

    v_halves = _half_masks((tk, LANES))
    one = (jnp.full((tq, 1), NEG, F32), jnp.zeros((tq, LANES), F32))
    c = lax.fori_loop(0, n_full, functools.partial(block, False), (one, one))
    c = lax.fori_loop(n_full, n_kb, functools.partial(block, True), c)
    acc0, acc1 = c[0][1], c[1][1]
    o_ref[0] = jnp.where(halves[0], acc0 / acc0[:, LANES - 1:LANES], acc1 / acc1[:, 0:1]).astype(o_ref.dtype)


def _flash_attention(mode, qb, kt, vb, extra, pair0, n_pairs):
    b, t, _ = qb.shape
    tq = FLASH_TQ
    tk = min(FLASH_TK, t)
    in_specs = [pl.BlockSpec((1, tq, LANES), lambda b_, p, i: (b_, i, pair0 + p)),
                pl.BlockSpec((1, LANES, t), lambda b_, p, i: (b_, pair0 + p, 0)),
                pl.BlockSpec((1, t, LANES), lambda b_, p, i: (b_, 0, pair0 + p))]
    if mode == "dsa":
        in_specs.append(pl.BlockSpec((1, tq, t), lambda b_, p, i: (b_, i, 0)))
    else:
        in_specs += [pl.BlockSpec((1, tq, N_HEADS), lambda b_, p, i: (b_, i, 0)),
                     pl.BlockSpec((1, N_HEADS, t), lambda b_, p, i: (b_, 0, 0))]
    return pl.pallas_call(
        functools.partial(_flash_kernel, mode, tk),
        grid=(b, n_pairs, t // tq),
        in_specs=in_specs,
        out_specs=pl.BlockSpec((1, tq, LANES), lambda b_, p, i: (b_, i, p)),
        out_shape=jax.ShapeDtypeStruct((b, t, n_pairs * LANES), BF16),
        compiler_params=_cparams(("parallel", "parallel", "parallel")),
        name=mode + "_attention",
    )(qb, kt, vb, *extra)


def _idx_scores(qcat, wix, kixt):
    rows = wix.shape[0]
    kp = _split3(kixt)
    k_cross = jnp.concatenate([jnp.zeros_like(kp[0]) if n is None else kp[n] for n in _PIECE_ROWS], axis=0)
    s_all = _dot(qcat[:, 0:IDX_DIM], kp[0]) + _dot(qcat, k_cross)
    score = None
    for hi in range(N_IDX_HEADS):
        term = wix[:, hi:hi + 1] * jnp.maximum(s_all[hi * rows:(hi + 1) * rows], 0.0)
        score = term if score is None else score + term
    return score


def _sort_key(score):
    bits = lax.bitcast_convert_type(score, I32)
    sign = bits >> 31
    return ((bits & jnp.int32(0x7FFFFFFF)) ^ sign) - sign


def _select_topk(keys_ref, out_ref, n_chunks, top):
    rows, width = keys_ref.shape
    total_chunks = width // SEL_CHUNK
    sub = SEL_CHUNK // LANES

    def count(pred, cand):
        def body(c, acc):
            base = c * SEL_CHUNK
            for s_ in range(sub):
                kc = keys_ref[:, pl.ds(pl.multiple_of(base + s_ * LANES, LANES), LANES)]
                acc = acc + jnp.where(pred(kc, cand), 1, 0)
            return acc
        acc = lax.fori_loop(0, n_chunks, body, jnp.zeros((rows, LANES), I32))
        return jnp.sum(acc, axis=1, keepdims=True)

    ge = lambda kc, cand: kc >= cand
    zero = jnp.zeros((rows, 1), I32)
    thr = jnp.where(count(ge, zero) >= top, zero, jnp.full((rows, 1), INT_MIN, I32))

    def bit_body(bi, thr):
        cand = thr | (jnp.int32(1) << (30 - bi))
        return jnp.where(count(ge, cand) >= top, cand, thr)

    thr = lax.fori_loop(0, 31, bit_body, thr)
    need = (top - count(lambda kc, cand: kc > cand, thr)).astype(F32)
    u = _tri(LANES, "le")

    def sel_body(c, seen):
        base = c * SEL_CHUNK
        for s_ in range(sub):
            off = pl.multiple_of(base + s_ * LANES, LANES)
            kc = keys_ref[:, pl.ds(off, LANES)]
            is_eq = kc == thr
            eq = jnp.where(is_eq, 1.0, 0.0)
            upto = _dot(eq.astype(BF16), u) + seen
            sel = jnp.logical_or(kc > thr, jnp.logical_and(is_eq, upto - eq < need))
            out_ref[:, pl.ds(off, LANES)] = jnp.where(sel, 1.0, 0.0).astype(out_ref.dtype)
            seen = seen + jnp.sum(eq, axis=1, keepdims=True)
        return seen

    lax.fori_loop(0, n_chunks, sel_body, jnp.zeros((rows, 1), F32))

    def zero_body(c, _):
        off = pl.multiple_of(c * SEL_CHUNK, SEL_CHUNK)
        out_ref[:, pl.ds(off, SEL_CHUNK)] = jnp.zeros((rows, SEL_CHUNK), out_ref.dtype)
        return 0

    lax.fori_loop(n_chunks, total_chunks, zero_body, 0)


def _prompt_select_kernel(top, qixt_ref, wixt_ref, kix_ref, sel_ref, keys_sc):
    tq = qixt_ref.shape[2]
    t = kix_ref.shape[1]
    i = pl.program_id(1)
    nq = N_IDX_HEADS * IDX_DIM
    wixt = wixt_ref[0]
    qpos = i * tq + lax.broadcasted_iota(I32, (1, tq), 1)
    n_chunks = (i * tq + tq + SEL_CHUNK - 1) // SEL_CHUNK

    def q_piece(n, hi):
        return qixt_ref[0, n * nq + hi * IDX_DIM:n * nq + (hi + 1) * IDX_DIM, :]

    q_main = jnp.concatenate([q_piece(0, hi) for hi in range(N_IDX_HEADS)], axis=1)
    q_cross = jnp.concatenate(
        [jnp.concatenate([jnp.zeros((IDX_DIM, tq), BF16) if n is None else q_piece(n, hi)
                          for n in _PIECE_ROWS], axis=0) for hi in range(N_IDX_HEADS)], axis=1)

    def chunk(c):
        return pl.ds(pl.multiple_of(c * SEL_CHUNK, SEL_CHUNK), SEL_CHUNK)

    def score_body(c, _):
        kix = kix_ref[0, chunk(c), :]
        s_all = _dot(kix[:, 0:IDX_DIM], q_main) + _dot(kix, q_cross)
        score = None
        for hi in range(N_IDX_HEADS):
            term = wixt[hi:hi + 1, :] * jnp.maximum(s_all[:, hi * tq:(hi + 1) * tq], 0.0)
            score = term if score is None else score + term
        kpos = c * SEL_CHUNK + lax.broadcasted_iota(I32, (SEL_CHUNK, 1), 0)
        keys_sc[chunk(c), :] = _sort_key(jnp.where(kpos <= qpos, score, -jnp.inf))
        return 0

    lax.fori_loop(0, n_chunks, score_body, 0)

    def count(pred, cand):
        def body(c, acc):
            hit = jnp.where(pred(keys_sc[chunk(c), :], cand), 1, 0)
            return acc + jnp.sum(hit.reshape(SEL_CHUNK // 8, 8, tq), axis=0)
        acc = lax.fori_loop(0, n_chunks, body, jnp.zeros((8, tq), I32))
        return jnp.sum(acc, axis=0, keepdims=True)

    ge = lambda kc, cand: kc >= cand
    zero = jnp.zeros((1, tq), I32)
    thr = jnp.where(count(ge, zero) >= top, zero, jnp.full((1, tq), INT_MIN, I32))

    def bit_body(bi, thr):
        cand = thr | (jnp.int32(1) << (30 - bi))
        return jnp.where(count(ge, cand) >= top, cand, thr)

    thr = lax.fori_loop(0, 31, bit_body, thr)
    need = (top - count(lambda kc, cand: kc > cand, thr)).astype(F32)
    lower = _tri(SEL_CHUNK, "ge")

    def sel_body(c, seen):
        kc = keys_sc[chunk(c), :]
        is_eq = kc == thr
        eq = jnp.where(is_eq, 1.0, 0.0)
        upto = _dot(lower, eq.astype(BF16)) + seen
        sel = jnp.logical_or(kc > thr, jnp.logical_and(is_eq, upto - eq < need))
        sel_ref[0, :, chunk(c)] = jnp.where(sel, 1.0, 0.0).T.astype(sel_ref.dtype)
        return seen + jnp.sum(eq, axis=0, keepdims=True)

    lax.fori_loop(0, n_chunks, sel_body, jnp.zeros((1, tq), F32))

    def zero_body(c, _):
        sel_ref[0, :, chunk(c)] = jnp.zeros((tq, SEL_CHUNK), sel_ref.dtype)
        return 0

    lax.fori_loop(n_chunks, t // SEL_CHUNK, zero_body, 0)


def _prompt_select(qixt, wixt, kixb):
    b, _, t = qixt.shape
    tq = SEL_TQ
    top = min(DSA_TOPK, t // 4)
    return pl.pallas_call(
        functools.partial(_prompt_select_kernel, top),
        grid=(b, t // tq),
        in_specs=[pl.BlockSpec((1, 3 * N_IDX_HEADS * IDX_DIM, tq), lambda b_, i: (b_, 0, i)),
                  pl.BlockSpec((1, 8, tq), lambda b_, i: (b_, 0, i)),
                  pl.BlockSpec((1, t, 4 * LANES), lambda b_, i: (b_, 0, 0))],
        out_specs=pl.BlockSpec((1, tq, t), lambda b_, i: (b_, i, 0)),
        out_shape=jax.ShapeDtypeStruct((b, t, t), BF16),
        scratch_shapes=[pltpu.VMEM((t, tq), I32)],
        compiler_params=_cparams(("parallel", "parallel")),
        name="prompt_select",
    )(qixt, wixt, kixb)


def _dec_scores_kernel(n_steps, past, pt_ref, qix_ref, wix_ref, *refs):
    page_refs = refs[:DEC_SCORE_PAGES]
    kn_ref, s_ref = refs[DEC_SCORE_PAGES:]
    j = pl.program_id(1)
    rows = wix_ref.shape[1]
    width = s_ref.shape[2]
    qpos = past + lax.broadcasted_iota(I32, (rows, 1), 0)
    kpos = j * width + lax.broadcasted_iota(I32, (1, width), 1)

    def emit(kixt):
        score = _idx_scores(qix_ref[0], wix_ref[0], kixt)
        s_ref[0] = jnp.where(kpos <= qpos, score, -jnp.inf)

    @pl.when(j < n_steps)
    def _():
        emit(jnp.concatenate([r[...] for r in page_refs], axis=1))

    @pl.when(j == n_steps)
    def _():
        emit(kn_ref[0])


def _dec_scores(page_table, qix, wix, kix_cache_t, kix_new_t, past):
    db, n_pages = page_table.shape
    rows = wix.shape[1]
    width = DEC_SCORE_PAGES * PAGE
    n_steps = n_pages // DEC_SCORE_PAGES

    def page_spec(k):
        return pl.BlockSpec((None, IDX_DIM, PAGE),
                            lambda b, j, pt: (pt[b, jnp.minimum(j, n_steps - 1) * DEC_SCORE_PAGES + k], 0, 0))

    grid_spec = pltpu.PrefetchScalarGridSpec(
        num_scalar_prefetch=1,
        grid=(db, n_steps + 1),
        in_specs=[pl.BlockSpec((1,) + qix.shape[1:], lambda b, j, pt: (b, 0, 0)),
                  pl.BlockSpec((1, rows, N_IDX_HEADS), lambda b, j, pt: (b, 0, 0))]
                 + [page_spec(k) for k in range(DEC_SCORE_PAGES)]
                 + [pl.BlockSpec((1, IDX_DIM, width), lambda b, j, pt: (b, 0, 0))],
        out_specs=pl.BlockSpec((1, rows, width), lambda b, j, pt: (b, 0, j)))
    return pl.pallas_call(
        functools.partial(_dec_scores_kernel, n_steps, past),
        grid_spec=grid_spec,
        out_shape=jax.ShapeDtypeStruct((db, rows, (n_steps + 1) * width), F32),
        compiler_params=_cparams(("parallel", "arbitrary")),
        name="dec_scores",
    )(page_table, qix, wix, *([kix_cache_t] * DEC_SCORE_PAGES), kix_new_t)


def _dec_select_kernel(top, s_ref, sel_ref, keys_sc):
    keys_sc[...] = _sort_key(s_ref[...])
    _select_topk(keys_sc, sel_ref, s_ref.shape[1] // SEL_CHUNK, top)


def _dec_select(scores, top):
    r, width = scores.shape
    tr = min(128, r)
    return pl.pallas_call(
        functools.partial(_dec_select_kernel, top),
        grid=(r // tr,),
        in_specs=[pl.BlockSpec((tr, width), lambda i: (i, 0))],
        out_specs=pl.BlockSpec((tr, width), lambda i: (i, 0)),
        out_shape=jax.ShapeDtypeStruct((r, width), BF16),
        scratch_shapes=[pltpu.VMEM((tr, width), I32)],
        compiler_params=_cparams(("parallel",)),
        name="dec_select",
    )(scores)


def _dec_attn_kernel(mode, n_steps, past, n_q, pt_ref, *refs):
    np_ = DEC_PAGES
    q_ref = refs[0]
    k_refs = refs[1:1 + np_]
    v_refs = refs[1 + np_:1 + 2 * np_]
    kn_ref, vn_ref = refs[1 + 2 * np_:3 + 2 * np_]
    rest = refs[3 + 2 * np_:]
    if mode == "even":
        sel_ref, seln_ref, o_ref, m_sc, l_sc, acc_sc, carry_sc = rest
    else:
        lf_refs = rest[:np_]
        lfn_ref, o_ref, m_sc, l_sc, acc_sc, carry_sc, sq_sc = rest[np_:]
    j = pl.program_id(1)
    n_sb = N_SB_HEADS * n_q if mode == "even" else 0
    row = lax.broadcasted_iota(I32, (DEC_ROWS, 1), 0)
    qpos = past + row % n_q

    @pl.when(j == 0)
    def _():
        m_sc[...] = jnp.full_like(m_sc, NEG)
        l_sc[...] = jnp.zeros_like(l_sc)
        acc_sc[...] = jnp.zeros_like(acc_sc)
        carry_sc[...] = jnp.zeros_like(carry_sc)

    def softmax_rows(s, valid, m_old, l_old):
        s = jnp.where(valid, s, NEG)
        m_new = jnp.maximum(m_old, jnp.max(s, axis=1, keepdims=True))
        a = jnp.exp2(m_old - m_new)
        pr = jnp.where(valid, jnp.exp2(s - m_new), 0.0)
        return pr, a, m_new, a * l_old + jnp.sum(pr, axis=1, keepdims=True)

    def step(kt, vt, key0, extra, first):
        w_ = kt.shape[1]
        s = _dot(q_ref[0], kt.astype(BF16))
        kpos = key0 + lax.broadcasted_iota(I32, (1, w_), 1)
        if mode == "even":
            z = s[0:n_sb]
            vis = kpos < qpos[0:n_sb]
            ls = jnp.minimum(z, 0.0) - jnp.log2(1.0 + jnp.exp2(-jnp.abs(z)))
            lm = jnp.where(vis, ls - z, 0.0)
            rest_, carry = _suffix_sums(lm, carry_sc[...])
            carry_sc[...] = carry
            w = jnp.where(vis, jnp.exp2(ls + rest_), 0.0)
            r_ = lax.broadcasted_iota(I32, (DEC_ROWS - n_sb, DEC_QPAD), 0)
            c_ = lax.broadcasted_iota(I32, (DEC_ROWS - n_sb, DEC_QPAD), 1)
            spread = jnp.where(c_ == r_ % n_q, 1.0, 0.0).astype(BF16)
            selx = _dot(spread, extra) > 0.5
            valid = jnp.logical_and(kpos <= qpos[n_sb:], selx)
            pr, a, m_new, l_new = softmax_rows(s[n_sb:], valid, m_sc[n_sb:], l_sc[n_sb:])
            m_sc[n_sb:] = m_new
            l_sc[n_sb:] = l_new
            p_all = jnp.concatenate([w, pr], axis=0)
            a_all = jnp.concatenate([jnp.ones((n_sb, 1), F32), a], axis=0)
        else:
            suf, carry = _suffix_sums(extra, carry_sc[...])
            carry_sc[...] = carry
            r_ = lax.broadcasted_iota(I32, (DEC_ROWS, N_HEADS), 0)
            c_ = lax.broadcasted_iota(I32, (DEC_ROWS, N_HEADS), 1)
            spread = jnp.where(c_ == r_ // n_q, 1.0, 0.0).astype(BF16)
            suf_rows = _dot3_left(spread, suf)
            if first:
                lane = lax.broadcasted_iota(I32, (1, w_), 1)
                sq_sc[...] = jnp.sum(jnp.where(lane == row % n_q, suf_rows, 0.0), axis=1, keepdims=True)
            logit = s + (suf_rows - sq_sc[...]) * LOG2E
            p_all, a_all, m_new, l_new = softmax_rows(logit, kpos <= qpos, m_sc[...], l_sc[...])
            m_sc[...] = m_new
            l_sc[...] = l_new
        acc_sc[...] = acc_sc[...] * _col_to_row(a_all) + _dot_nt(vt.astype(BF16), p_all.astype(BF16))

    @pl.when(j == 0)
    def _():
        extra = seln_ref[0] if mode == "even" else lfn_ref[0]
        step(kn_ref[0], vn_ref[0], past, extra, True)

    @pl.when(j > 0)
    def _():
        kt = jnp.concatenate([r[...] for r in k_refs], axis=1)
        vt = jnp.concatenate([r[...] for r in v_refs], axis=1)
        if mode == "even":
            extra = sel_ref[0]
        else:
            extra = jnp.concatenate([r[...] for r in lf_refs], axis=1)
        step(kt, vt, past - j * (np_ * PAGE), extra, False)

    @pl.when(j == n_steps)
    def _():
        l = jnp.where(row < n_sb, 1.0, l_sc[...])
        hr = lax.broadcasted_iota(I32, (D_MODEL, DEC_ROWS), 0) // HEAD_DIM
        hc = lax.broadcasted_iota(I32, (D_MODEL, DEC_ROWS), 1) // n_q
        own = jnp.where(hr == hc, acc_sc[...], 0.0).reshape(N_HEADS, HEAD_DIM, DEC_ROWS)
        o_ref[0] = jnp.sum(own, axis=0) / _col_to_row(l)


def _dec_attention(mode, page_table, qbd, k_cache_t, v_cache_t, k_new_t, v_new_t, extra, past, n_q):
    db, n_pages = page_table.shape
    np_ = DEC_PAGES
    n_steps = n_pages // np_
    width = np_ * PAGE
    per_b = lambda b, j, pt: (b, 0, 0)

    def page_spec(k, feat):
        return pl.BlockSpec((None, feat, PAGE),
                            lambda b, j, pt: (pt[b, n_pages - jnp.maximum(j, 1) * np_ + k], 0, 0))

    in_specs = ([pl.BlockSpec((1, DEC_ROWS, D_MODEL), per_b)]
                + [page_spec(k, D_MODEL) for k in range(np_)]
                + [page_spec(k, D_MODEL) for k in range(np_)]
                + [pl.BlockSpec((1, D_MODEL, PAGE), per_b), pl.BlockSpec((1, D_MODEL, PAGE), per_b)])
    scratch = [pltpu.VMEM((DEC_ROWS, 1), F32), pltpu.VMEM((DEC_ROWS, 1), F32),
               pltpu.VMEM((D_MODEL, DEC_ROWS), F32)]
    if mode == "even":
        (sel,) = extra
        rows = sel.shape[1]
        in_specs += [pl.BlockSpec((1, rows, width), lambda b, j, pt: (b, 0, n_steps - jnp.maximum(j, 1))),
                     pl.BlockSpec((1, rows, PAGE), per_b)]
        operands = [sel, sel[:, :, past:past + PAGE]]
        scratch.append(pltpu.VMEM((N_SB_HEADS * n_q, 1), F32))
    else:
        logf_cache_t, logf_new_t = extra
        in_specs += [page_spec(k, N_HEADS) for k in range(np_)]
        in_specs.append(pl.BlockSpec((1, N_HEADS, PAGE), per_b))
        operands = [logf_cache_t] * np_ + [logf_new_t]
        scratch += [pltpu.VMEM((N_HEADS, 1), F32), pltpu.VMEM((DEC_ROWS, 1), F32)]
    grid_spec = pltpu.PrefetchScalarGridSpec(
        num_scalar_prefetch=1,
        grid=(db, n_steps + 1),
        in_specs=in_specs,
        out_specs=pl.BlockSpec((1, HEAD_DIM, DEC_ROWS), per_b),
        scratch_shapes=scratch)
    return pl.pallas_call(
        functools.partial(_dec_attn_kernel, mode, n_steps, past, n_q),
        grid_spec=grid_spec,
        out_shape=jax.ShapeDtypeStruct((db, HEAD_DIM, DEC_ROWS), F32),
        compiler_params=_cparams(("parallel", "arbitrary")),
        name="dec_attention_" + mode,
    )(page_table, qbd, *([k_cache_t] * np_), *([v_cache_t] * np_), k_new_t, v_new_t, *operands)


def _rope_tables(pos):
    half = HEAD_DIM // 2
    inv = ROPE_THETA ** (-jnp.arange(half, dtype=F32) / half)
    ang = pos.astype(F32)[:, None] * inv
    cos, sin = jnp.cos(ang), jnp.sin(ang)
    reps = W_HALF // HEAD_DIM
    return (jnp.tile(jnp.concatenate([cos, cos], axis=-1), (1, reps)),
            jnp.tile(jnp.concatenate([-sin, sin], axis=-1), (1, reps)))


def _heads_view(xt):
    g, _, t = xt.shape
    return xt.reshape(g, N_HEADS, HEAD_DIM, t).transpose(0, 3, 1, 2)


def _new_block(xt, db, n_q, width):
    f = xt.shape[0]
    return jnp.pad(xt.reshape(f, db, n_q).transpose(1, 0, 2), ((0, 0), (0, 0), (0, width - n_q)))


def _block_diag_queries(qb, db, n_q):
    q = qb.reshape(db, n_q, N_HEADS, HEAD_DIM).transpose(0, 2, 1, 3)
    eye = jnp.eye(N_HEADS, dtype=qb.dtype)
    q = q[:, :, :, None, :] * eye[None, :, None, :, None]
    q = q.reshape(db, N_HEADS * n_q, D_MODEL)
    return jnp.pad(q, ((0, 0), (0, DEC_ROWS - N_HEADS * n_q), (0, 0)))


def _query_piece_blocks(qix3, db, n_q):
    p = qix3.reshape(db, n_q, 3, N_IDX_HEADS, IDX_DIM)
    order = [n for pair in _PIECE_PAIRS for n in pair]
    blocks = jnp.stack([p[:, :, n] for n in order], axis=3)
    blocks = blocks.transpose(0, 2, 1, 3, 4).reshape(db, N_IDX_HEADS, n_q, 8 * IDX_DIM)
    blocks = jnp.pad(blocks, ((0, 0), (0, 0), (0, DEC_QPAD - n_q), (0, 0)))
    return blocks.reshape(db, N_IDX_HEADS * DEC_QPAD, 8 * IDX_DIM)


def _rows_token(o, db, n_q):
    o = o[:, :, :N_HEADS * n_q].reshape(db, HEAD_DIM, N_HEADS, n_q)
    return o.transpose(0, 3, 2, 1).reshape(1, db * n_q, D_MODEL)


def kernel(x_prompt, x_sample, cache_k_even, cache_v_even, cache_kidx_even, cache_k_odd, cache_v_odd,
           cache_logf_odd, page_table, c_prompt, c_sample, w_ada, b_ada, ln_g, ln_b, ffn_w1, ffn_w3,
           ffn_w2, w_in_even, w_out_even, w_in_odd, b_forget, w_out_odd):
    b, t, _ = x_prompt.shape
    db, n_q, _ = x_sample.shape
    n_pages = page_table.shape[1]
    past = n_pages * PAGE
    pool = cache_k_even.shape[1]

    n_c = b + db
    c_all = jnp.pad(jnp.concatenate([c_prompt, c_sample], axis=0), ((0, (-n_c) % 8), (0, 0)))
    mod = _ada_mod(c_all, w_ada, b_ada).reshape(DEPTH, -1, N_SUB, 3, D_MODEL)

    def mods(layer, sub):
        m = mod[layer, :, sub]
        mp = [m[:b, i].reshape(b, 1, D_MODEL) for i in range(3)]
        ms = [jnp.repeat(m[b:n_c, i], n_q, axis=0).reshape(1, db * n_q, D_MODEL) for i in range(3)]
        return mp, ms

    cos_p, sin_p = _rope_tables(jnp.tile(jnp.arange(t, dtype=I32), b))
    cos_s, sin_s = _rope_tables(jnp.tile(past + jnp.arange(n_q, dtype=I32), db))

    w_even = jnp.pad(w_in_even, ((0, 0), (0, 0), (0, P_EVEN_PAD - P_EVEN))).astype(BF16)
    w_odd = jnp.pad(w_in_odd, ((0, 0), (0, 0), (0, P_ODD_PAD - w_in_odd.shape[-1]))).astype(BF16)
    bf_pad = jnp.pad(b_forget, ((0, 0), (0, LANES - N_HEADS)))
    w1, w3, w2 = ffn_w1.astype(BF16), ffn_w3.astype(BF16), ffn_w2.astype(BF16)
    wo_even, wo_odd = w_out_even.astype(BF16), w_out_odd.astype(BF16)

    def cache_t(cache, j, feat):
        c = cache[j].reshape(pool, PAGE, feat)
        return c.transpose(0, 2, 1)

    top_s = min(DSA_TOPK, (past + n_q) // 4)
    score_w = DEC_SCORE_PAGES * PAGE

    xp = x_prompt
    xs = x_sample.reshape(1, db * n_q, D_MODEL)
    even_p, odd_p, even_s, odd_s = [], [], [], []
    for layer in range(DEPTH):
        j = layer // 2
        (mp0, ms0), (mp1, ms1), (mp2, ms2) = mods(layer, 0), mods(layer, 1), mods(layer, 2)
        ffn_a = (w1[layer, 0], w3[layer, 0], w2[layer, 0], ln_g[layer, 0], ln_b[layer, 0])
        xp = _ffn_sublayer(xp, *mp0, *ffn_a)
        xs = _ffn_sublayer(xs, *ms0, *ffn_a)
        if layer % 2 == 0:
            kt32, vt32, kixt32, qb, ktb, vb, _, _, kixb, qixt, wixt = _even_proj(
                xp, mp1[0], mp1[1], w_even[j], cos_p, sin_p)
            even_p.append((_heads_view(kt32), _heads_view(vt32), kixt32.transpose(0, 2, 1)))
            o_sb = _sb_attention(qb, ktb, vb)
            sel = _prompt_select(qixt, wixt, kixb)
            o_ds = _flash_attention("dsa", qb, ktb, vb, (sel,), N_SB_PAIRS, N_PAIRS - N_SB_PAIRS)
            xp = _out_proj((o_sb, o_ds), (wo_even[j, :W_HALF], wo_even[j, W_HALF:]), xp, mp1[2],
                           ln_g[layer, 1], ln_b[layer, 1])
            kt32, vt32, kixt32, qb, _, _, qix, wix, _, _, _ = _even_proj(
                xs, ms1[0], ms1[1], w_even[j], cos_s, sin_s)
            even_s.append((kt32[0].T.reshape(db, n_q, N_HEADS, HEAD_DIM),
                           vt32[0].T.reshape(db, n_q, N_HEADS, HEAD_DIM),
                           kixt32[0].T.reshape(db, n_q, IDX_DIM)))
            pad_q = lambda a: jnp.pad(a.reshape(db, n_q, -1), ((0, 0), (0, DEC_QPAD - n_q), (0, 0)))
            scores = _dec_scores(page_table, _query_piece_blocks(qix[0], db, n_q), pad_q(wix),
                                 cache_t(cache_kidx_even, j, IDX_DIM),
                                 _new_block(kixt32[0], db, n_q, score_w), past)
            sel = _dec_select(scores.reshape(db * DEC_QPAD, -1), top_s).reshape(db, DEC_QPAD, -1)
            o = _dec_attention(
                "even", page_table, _block_diag_queries(qb[0], db, n_q),
                cache_t(cache_k_even, j, D_MODEL), cache_t(cache_v_even, j, D_MODEL),
                _new_block(kt32[0], db, n_q, PAGE), _new_block(vt32[0], db, n_q, PAGE),
                (sel,), past, n_q)
            xs = _out_proj((_rows_token(o, db, n_q).astype(BF16),), (wo_even[j],), xs, ms1[2],
                           ln_g[layer, 1], ln_b[layer, 1])
        else:
            kt32, vt32, lft, qb, ktb, vb, fq, ft = _odd_proj(xp, mp1[0], mp1[1], w_odd[j], bf_pad[j:j + 1])
            odd_p.append((_heads_view(kt32), _heads_view(vt32), lft.transpose(0, 2, 1)))
            o = _flash_attention("fox", qb, ktb, vb, (fq, ft), 0, N_PAIRS)
            xp = _out_proj((o,), (wo_odd[j],), xp, mp1[2], ln_g[layer, 1], ln_b[layer, 1])
            kt32, vt32, lft, qb, _, _, _, _ = _odd_proj(xs, ms1[0], ms1[1], w_odd[j], bf_pad[j:j + 1])
            odd_s.append((kt32[0].T.reshape(db, n_q, N_HEADS, HEAD_DIM),
                          vt32[0].T.reshape(db, n_q, N_HEADS, HEAD_DIM),
                          lft[0].T.reshape(db, n_q, N_HEADS)))
            o = _dec_attention(
                "odd", page_table, _block_diag_queries(qb[0], db, n_q),
                cache_t(cache_k_odd, j, D_MODEL), cache_t(cache_v_odd, j, D_MODEL),
                _new_block(kt32[0], db, n_q, PAGE), _new_block(vt32[0], db, n_q, PAGE),
                (cache_t(cache_logf_odd, j, N_HEADS), _new_block(lft[0], db, n_q, PAGE)), past, n_q)
            xs = _out_proj((_rows_token(o, db, n_q).astype(BF16),), (wo_odd[j],), xs, ms1[2],
                           ln_g[layer, 1], ln_b[layer, 1])
        ffn_b = (w1[layer, 1], w3[layer, 1], w2[layer, 1], ln_g[layer, 2], ln_b[layer, 2])
        xp = _ffn_sublayer(xp, *mp2, *ffn_b)
        xs = _ffn_sublayer(xs, *ms2, *ffn_b)

    def stack(rows, i):
        return jnp.stack([r[i] for r in rows])

    return (xp, xs.reshape(db, n_q, D_MODEL),
            stack(even_p, 0), stack(even_p, 1), stack(even_p, 2),
            stack(odd_p, 0), stack(odd_p, 1), stack(odd_p, 2),
            stack(even_s, 0), stack(even_s, 1), stack(even_s, 2),
            stack(odd_s, 0), stack(odd_s, 1), stack(odd_s, 2))
```
